```python
import jax, jax.numpy as jnp
from jax import lax
import numpy as np

D_MODEL = 1024
BATCH = 2
SEQ = 8192
DEPTH = 2
DEC_BATCH = 32
DEC_SEQ = 4
PAST_LEN = 8192
PAGE_SIZE = 128

N_HEADS = 8
HEAD_DIM = 64
ATT_W = N_HEADS * HEAD_DIM
CONV_CH = D_MODEL - ATT_W
CONV_W = 31
D_FF = 2816
PLE_DIM = 256
Q_BLOCK = 128
EPS = 1e-6
N_MIX_IN = 3 * ATT_W + N_HEADS + 2 * CONV_CH
N_PAGES = PAST_LEN // PAGE_SIZE
N_POOL = DEC_BATCH * N_PAGES + max(1, (DEC_BATCH * N_PAGES) // 4)

kernel_name = "fox_conformer_macaron_hybrid_step"


def rmsnorm(x, g):
    xf = x.astype(jnp.float32)
    y = xf * lax.rsqrt(jnp.mean(xf * xf, axis=-1, keepdims=True) + EPS)
    return (y * g.astype(jnp.float32)).astype(x.dtype)


def layernorm(x, g, b):
    xf = x.astype(jnp.float32)
    mu = jnp.mean(xf, axis=-1, keepdims=True)
    var = jnp.mean(jnp.square(xf - mu), axis=-1, keepdims=True)
    y = (xf - mu) * lax.rsqrt(var + EPS)
    return (y * g.astype(jnp.float32) + b.astype(jnp.float32)).astype(x.dtype)


def swiglu_ffn(h, w_in, w_out):
    gate, up = jnp.split(h @ w_in, 2, axis=-1)
    return (jax.nn.silu(gate) * up) @ w_out


def mix_projection(h, w_in, b_f, q_g, k_g):
    B, T, _ = h.shape
    z = h @ w_in
    q, k, v, f, c = jnp.split(z, [ATT_W, 2 * ATT_W, 3 * ATT_W, 3 * ATT_W + N_HEADS], axis=-1)
    q = rmsnorm(q.reshape(B, T, N_HEADS, HEAD_DIM), q_g)
    k = rmsnorm(k.reshape(B, T, N_HEADS, HEAD_DIM), k_g)
    v = v.reshape(B, T, N_HEADS, HEAD_DIM)
    logf = jax.nn.log_sigmoid(f.astype(jnp.float32) + b_f.astype(jnp.float32))
    a, g = jnp.split(c, 2, axis=-1)
    u = a * jax.nn.sigmoid(g)
    return q, k, v, logf, u


def fox_attend(q, k, v, cq, ck, qpos, kpos):
    s = jnp.einsum('bqhd,bkhd->bhqk', q, k, preferred_element_type=jnp.float32) * (HEAD_DIM ** -0.5)
    bias = jnp.transpose(cq, (0, 2, 1))[..., None] - jnp.transpose(ck, (0, 2, 1))[:, :, None, :]
    logits = jnp.where(kpos[None, :] <= qpos[:, None], s + bias, -jnp.inf)
    p = jax.nn.softmax(logits, axis=-1)
    return jnp.einsum('bhqk,bkhd->bqhd', p.astype(v.dtype), v)


def fox_prompt(q, k, v, logf):
    B, T, H, HD = q.shape
    nb = T // Q_BLOCK
    c = jnp.cumsum(logf, axis=1)
    kpos = jnp.arange(T)
    qb = q.reshape(B, nb, Q_BLOCK, H, HD).swapaxes(0, 1)
    cqb = c.reshape(B, nb, Q_BLOCK, H).swapaxes(0, 1)

    def one_block(args):
        qi, ci, i = args
        qpos = i * Q_BLOCK + jnp.arange(Q_BLOCK)
        return fox_attend(qi, k, v, ci, c, qpos, kpos)

    out = lax.map(one_block, (qb, cqb, jnp.arange(nb)))
    return out.swapaxes(0, 1).reshape(B, T, ATT_W)


def fox_sample(q, k_new, v_new, logf_new, k_past, v_past, logf_past):
    B, S = q.shape[0], q.shape[1]
    P = k_past.shape[1]
    k = jnp.concatenate([k_past, k_new.astype(k_past.dtype)], axis=1)
    v = jnp.concatenate([v_past, v_new.astype(v_past.dtype)], axis=1)
    c = jnp.cumsum(jnp.concatenate([logf_past.astype(jnp.float32), logf_new], axis=1), axis=1)
    out = fox_attend(q, k, v, c[:, P:], c, P + jnp.arange(S), jnp.arange(P + S))
    return out.reshape(B, S, ATT_W).astype(q.dtype)


def conv_module(u_ext, w, b, g, beta):
    y = lax.conv_general_dilated(u_ext, w[:, None, :].astype(u_ext.dtype), window_strides=(1,),
                                 padding='VALID', dimension_numbers=('NWC', 'WIO', 'NWC'),
                                 feature_group_count=CONV_CH) + b.astype(u_ext.dtype)
    return jax.nn.silu(layernorm(y, g, beta))


def decoder_layer(x, p, past, w):
    (ffn1_norm, ffn1_w_in, ffn1_w_out, mix_norm, w_mix_in, b_forget, q_norm, k_norm,
     conv_w, conv_b, conv_ln_g, conv_ln_b, w_mix_out, ffn2_norm, ffn2_w_in, ffn2_w_out,
     ple_norm, w_ple_gate, w_ple_proj) = w
    x = x + 0.5 * swiglu_ffn(rmsnorm(x, ffn1_norm), ffn1_w_in, ffn1_w_out)
    q, k, v, logf, u = mix_projection(rmsnorm(x, mix_norm), w_mix_in, b_forget, q_norm, k_norm)
    if past is None:
        attn = fox_prompt(q, k, v, logf)
        u_ext = jnp.pad(u, ((0, 0), (CONV_W - 1, 0), (0, 0)))
    else:
        k_past, v_past, logf_past, conv_state = past
        attn = fox_sample(q, k, v, logf, k_past, v_past, logf_past)
        u_ext = jnp.concatenate([conv_state.astype(u.dtype), u], axis=1)
    conv = conv_module(u_ext, conv_w, conv_b, conv_ln_g, conv_ln_b)
    x = x + jnp.concatenate([attn.astype(x.dtype), conv.astype(x.dtype)], axis=-1) @ w_mix_out
    x = x + 0.5 * swiglu_ffn(rmsnorm(x, ffn2_norm), ffn2_w_in, ffn2_w_out)
    gate = jax.nn.sigmoid(rmsnorm(x, ple_norm) @ w_ple_gate)
    x = x + gate * (p.astype(x.dtype) @ w_ple_proj)
    return x, (k, v, logf, u_ext[:, -(CONV_W - 1):])


def setup_inputs(seed: int = 0) -> dict:
    key = jax.random.key(seed)
    ks = jax.random.split(key, 32)
    f32 = jnp.float32

    def nrm(k, shape, scale):
        return jax.random.normal(k, shape, f32) * scale

    perm = jax.random.permutation(ks[0], N_POOL)
    page_table = perm[:DEC_BATCH * N_PAGES].reshape(DEC_BATCH, N_PAGES).astype(jnp.int32)
    return {
        "x_prompt": nrm(ks[1], (BATCH, SEQ, D_MODEL), 1.0),
        "x_sample": nrm(ks[2], (DEC_BATCH, DEC_SEQ, D_MODEL), 1.0),
        "p_prompt": nrm(ks[3], (DEPTH, BATCH, SEQ, PLE_DIM), 1.0),
        "p_sample": nrm(ks[4], (DEPTH, DEC_BATCH, DEC_SEQ, PLE_DIM), 1.0),
        "cache_k": nrm(ks[5], (DEPTH, N_POOL, PAGE_SIZE, N_HEADS, HEAD_DIM), 1.0),
        "cache_v": nrm(ks[6], (DEPTH, N_POOL, PAGE_SIZE, N_HEADS, HEAD_DIM), 1.0),
        "cache_logf": jax.nn.log_sigmoid(2.5 + nrm(ks[7], (DEPTH, N_POOL, PAGE_SIZE, N_HEADS), 1.0)),
        "state_conv": nrm(ks[8], (DEPTH, DEC_BATCH, CONV_W - 1, CONV_CH), 0.5),
        "page_table": page_table,
        "ffn1_norm": 1.0 + nrm(ks[9], (DEPTH, D_MODEL), 0.02),
        "ffn1_w_in": nrm(ks[10], (DEPTH, D_MODEL, 2 * D_FF), D_MODEL ** -0.5),
        "ffn1_w_out": nrm(ks[11], (DEPTH, D_FF, D_MODEL), D_FF ** -0.5),
        "mix_norm": 1.0 + nrm(ks[12], (DEPTH, D_MODEL), 0.02),
        "w_mix_in": nrm(ks[13], (DEPTH, D_MODEL, N_MIX_IN), D_MODEL ** -0.5),
        "b_forget": 2.5 + nrm(ks[14], (DEPTH, N_HEADS), 0.5),
        "q_norm": 1.0 + nrm(ks[15], (DEPTH, HEAD_DIM), 0.02),
        "k_norm": 1.0 + nrm(ks[16], (DEPTH, HEAD_DIM), 0.02),
        "conv_w": nrm(ks[17], (DEPTH, CONV_W, CONV_CH), CONV_W ** -0.5),
        "conv_b": nrm(ks[18], (DEPTH, CONV_CH), 0.02),
        "conv_ln_g": 1.0 + nrm(ks[19], (DEPTH, CONV_CH), 0.02),
        "conv_ln_b": nrm(ks[20], (DEPTH, CONV_CH), 0.02),
        "w_mix_out": nrm(ks[21], (DEPTH, ATT_W + CONV_CH, D_MODEL), (ATT_W + CONV_CH) ** -0.5),
        "ffn2_norm": 1.0 + nrm(ks[22], (DEPTH, D_MODEL), 0.02),
        "ffn2_w_in": nrm(ks[23], (DEPTH, D_MODEL, 2 * D_FF), D_MODEL ** -0.5),
        "ffn2_w_out": nrm(ks[24], (DEPTH, D_FF, D_MODEL), D_FF ** -0.5),
        "ple_norm": 1.0 + nrm(ks[25], (DEPTH, D_MODEL), 0.02),
        "w_ple_gate": nrm(ks[26], (DEPTH, D_MODEL, D_MODEL), D_MODEL ** -0.5),
        "w_ple_proj": nrm(ks[27], (DEPTH, PLE_DIM, D_MODEL), PLE_DIM ** -0.5),
    }


def reference(x_prompt, x_sample, p_prompt, p_sample, cache_k, cache_v, cache_logf, state_conv,
              page_table, ffn1_norm, ffn1_w_in, ffn1_w_out, mix_norm, w_mix_in, b_forget,
              q_norm, k_norm, conv_w, conv_b, conv_ln_g, conv_ln_b, w_mix_out, ffn2_norm,
              ffn2_w_in, ffn2_w_out, ple_norm, w_ple_gate, w_ple_proj):
    xp, xs = x_prompt, x_sample
    nb = page_table.shape[0]
    kp, vp, fp, cp = [], [], [], []
    ksl, vsl, fsl, csl = [], [], [], []
    for l in range(DEPTH):
        w_l = (ffn1_norm[l], ffn1_w_in[l], ffn1_w_out[l], mix_norm[l], w_mix_in[l], b_forget[l],
               q_norm[l], k_norm[l], conv_w[l], conv_b[l], conv_ln_g[l], conv_ln_b[l],
               w_mix_out[l], ffn2_norm[l], ffn2_w_in[l], ffn2_w_out[l], ple_norm[l],
               w_ple_gate[l], w_ple_proj[l])
        xp, (k_, v_, f_, c_) = decoder_layer(xp, p_prompt[l], None, w_l)
        kp.append(k_); vp.append(v_); fp.append(f_); cp.append(c_)
        past = (cache_k[l][page_table].reshape(nb, -1, N_HEADS, HEAD_DIM),
                cache_v[l][page_table].reshape(nb, -1, N_HEADS, HEAD_DIM),
                cache_logf[l][page_table].reshape(nb, -1, N_HEADS),
                state_conv[l])
        xs, (k_, v_, f_, c_) = decoder_layer(xs, p_sample[l], past, w_l)
        ksl.append(k_); vsl.append(v_); fsl.append(f_); csl.append(c_)
    return (xp, xs, jnp.stack(kp), jnp.stack(vp), jnp.stack(fp), jnp.stack(cp),
            jnp.stack(ksl), jnp.stack(vsl), jnp.stack(fsl), jnp.stack(csl))
```

```python
import jax
import jax.numpy as jnp
from jax import lax
from jax.experimental import pallas as pl
from jax.experimental.pallas import tpu as pltpu

F32 = jnp.float32
BF16 = jnp.bfloat16

EPS = 1e-6
N_HEADS = 8
HEAD_DIM = 64
ATT_W = N_HEADS * HEAD_DIM
CONV_W = 31
LANES = 128
SUBLANES = 8
V7X_VMEM_BYTES = 64 * 2 ** 20
VMEM_LIMIT = V7X_VMEM_BYTES - 8 * 2 ** 20
NEG = -1e30
SKIP_THRESH = 110.0

ROW_TILE = 512
ATT_TILE = 512
CONV_HALO = 32
CONV_CHUNK = 32
PAGES_PER_STEP = 8


def _params(sem):
    return pltpu.CompilerParams(dimension_semantics=sem, vmem_limit_bytes=VMEM_LIMIT)


def _rms(x, g):
    return x * lax.rsqrt(jnp.mean(x * x, axis=-1, keepdims=True) + EPS) * g


def _sigmoid(x):
    return 1.0 / (1.0 + jnp.exp(-x))


def _silu(x):
    return x * _sigmoid(x)


def _log_sigmoid(x):
    return -(jnp.maximum(-x, 0.0) + jnp.log1p(jnp.exp(-jnp.abs(x))))


def _dot(a, b):
    return jnp.dot(a, b, preferred_element_type=F32)


def _dot_nt(a, b):
    return lax.dot_general(a, b, (((1,), (1,)), ((), ())), preferred_element_type=F32)


def _dot_tn(a, b):
    return lax.dot_general(a, b, (((0,), (0,)), ((), ())), preferred_element_type=F32)


def _split3(x):
    hi = x.astype(BF16)
    r = x - hi.astype(F32)
    mid = r.astype(BF16)
    lo = (r - mid.astype(F32)).astype(BF16)
    return hi, mid, lo


def _ffn_kernel(x_ref, g_ref, wg_ref, wu_ref, wo_ref, o_ref, h_ref, acc_ref):
    f = pl.program_id(1)

    @pl.when(f == 0)
    def _():
        h_ref[...] = _rms(x_ref[...], g_ref[...]).astype(BF16)

    h = h_ref[...]
    gate = _dot(h, wg_ref[...])
    up = _dot(h, wu_ref[...])
    part = _dot((_silu(gate) * up).astype(BF16), wo_ref[...])

    @pl.when(f == 0)
    def _():
        acc_ref[...] = part

    @pl.when(f > 0)
    def _():
        acc_ref[...] += part

    @pl.when(f == pl.num_programs(1) - 1)
    def _():
        o_ref[...] = x_ref[...] + 0.5 * acc_ref[...]


def _ffn(x, g, w_in, w_out, tm):
    m, d = x.shape
    dff = w_out.shape[0]
    tf = dff // 2 if (dff // 2) % LANES == 0 else dff
    nf = dff // tf
    return pl.pallas_call(
        _ffn_kernel,
        name="ffn",
        grid=(m // tm, nf),
        in_specs=[
            pl.BlockSpec((tm, d), lambda i, f: (i, 0)),
            pl.BlockSpec((1, d), lambda i, f: (0, 0)),
            pl.BlockSpec((d, tf), lambda i, f: (0, f)),
            pl.BlockSpec((d, tf), lambda i, f: (0, f + nf)),
            pl.BlockSpec((tf, d), lambda i, f: (f, 0)),
        ],
        out_specs=pl.BlockSpec((tm, d), lambda i, f: (i, 0)),
        out_shape=jax.ShapeDtypeStruct((m, d), F32),
        scratch_shapes=[pltpu.VMEM((tm, d), BF16), pltpu.VMEM((tm, d), F32)],
        compiler_params=_params(("arbitrary", "arbitrary")),
    )(x, g.reshape(1, d), w_in, w_in, w_out)


def _lane_cumsum(x):
    n = x.shape[1]
    idx = lax.broadcasted_iota(jnp.int32, x.shape, 1)
    s = 1
    while s < n:
        x = x + jnp.where(idx >= s, pltpu.roll(x, s, axis=1), 0.0)
        s *= 2
    return x


def _head_norm(t, gmat, g):
    tt = t * t
    hi = tt.astype(BF16)
    lo = (tt - hi.astype(F32)).astype(BF16)
    ms = _dot(hi, gmat) + _dot(lo, gmat)
    return t * lax.rsqrt(ms + EPS) * g


def _mixin_kernel(x_ref, g_ref, wqkv_ref, wf_ref, wc_ref, bf_ref, qg_ref, kg_ref, gmat_ref,
                  qb_ref, kb_ref, vb_ref, k_ref, v_ref, lf_ref, ct_ref, u_ref, carry_ref):
    i = pl.program_id(1)
    h = _rms(x_ref[0], g_ref[...]).astype(BF16)

    z = _dot(h, wqkv_ref[...])
    gmat = gmat_ref[...]
    qn = _head_norm(z[:, :ATT_W], gmat, qg_ref[...])
    kn = _head_norm(z[:, ATT_W:2 * ATT_W], gmat, kg_ref[...])
    v = z[:, 2 * ATT_W:]
    qb_ref[0] = (qn * (HEAD_DIM ** -0.5)).astype(BF16)
    kb_ref[0] = kn.astype(BF16)
    vb_ref[0] = v.astype(BF16)
    k_ref[0] = kn
    v_ref[0] = v

    lf = _log_sigmoid(_dot(h, wf_ref[...]) + bf_ref[...])
    lf_ref[0] = lf[:, :N_HEADS]

    @pl.when(i == 0)
    def _():
        carry_ref[...] = jnp.zeros_like(carry_ref)

    c = _lane_cumsum(lf.T[:N_HEADS, :]) + carry_ref[:, :1]
    ct_ref[0] = c
    carry_ref[...] = jnp.broadcast_to(c[:, -1:], carry_ref.shape)

    zc = _dot(h, wc_ref[...])
    cw = zc.shape[1] // 2
    u_ref[0] = zc[:, :cw] * _sigmoid(zc[:, cw:])


def _mixin(x, g, wqkv, wf, wc, bfp, qg, kg, gmat, tm):
    b, t, d = x.shape
    cw = wc.shape[1] // 2
    const = lambda shape: pl.BlockSpec(shape, lambda bi, i: (0,) * len(shape))
    row = lambda w: pl.BlockSpec((1, tm, w), lambda bi, i: (bi, i, 0))
    return pl.pallas_call(
        _mixin_kernel,
        name="mixin",
        grid=(b, t // tm),
        in_specs=[row(d), const((1, d)), const(wqkv.shape), const(wf.shape), const(wc.shape),
                  const((1, LANES)), const((1, ATT_W)), const((1, ATT_W)), const(gmat.shape)],
        out_specs=[row(ATT_W), row(ATT_W), row(ATT_W), row(ATT_W), row(ATT_W), row(N_HEADS),
                   pl.BlockSpec((1, N_HEADS, tm), lambda bi, i: (bi, 0, i)), row(cw)],
        out_shape=[jax.ShapeDtypeStruct((b, t, ATT_W), BF16)] * 3
        + [jax.ShapeDtypeStruct((b, t, ATT_W), F32)] * 2
        + [jax.ShapeDtypeStruct((b, t, N_HEADS), F32),
           jax.ShapeDtypeStruct((b, N_HEADS, t), F32),
           jax.ShapeDtypeStruct((b, t, cw), F32)],
        scratch_shapes=[pltpu.VMEM((N_HEADS, LANES), F32)],
        compiler_params=_params(("arbitrary", "arbitrary")),
    )(x, g.reshape(1, d), wqkv, wf, wc, bfp, qg, kg, gmat)


def _attn_kernel(sb_ref, q_ref, k_ref, v_ref, c_ref, o_ref, m_ref, l_ref, acc_ref):
    i = pl.program_id(2)
    tq = q_ref.shape[1]
    q = q_ref[0]
    lane = lax.broadcasted_iota(jnp.int32, q.shape, 1)
    first = lane < HEAD_DIM
    q_heads = (jnp.where(first, q, jnp.zeros_like(q)), jnp.where(first, jnp.zeros_like(q), q))
    q0 = pl.multiple_of(i * tq, tq)
    c_base = c_ref[0, 0, :, pl.ds(q0, LANES)][:, :1]
    s_bound = sb_ref[0]

    m_ref[...] = jnp.full_like(m_ref, NEG)
    l_ref[...] = jnp.zeros_like(l_ref)
    acc_ref[...] = jnp.zeros_like(acc_ref)

    def step(j, diagonal):
        k0 = pl.multiple_of(j * tq, tq)
        kb = k_ref[0, pl.ds(k0, tq), :]
        vb = v_ref[0, pl.ds(k0, tq), :]
        ck = c_ref[0, 0, :, pl.ds(k0, tq)] - c_base
        for hh in range(2):
            s = _dot_nt(q_heads[hh], kb) - ck[hh:hh + 1, :]
            if diagonal:
                r = lax.broadcasted_iota(jnp.int32, s.shape, 0)
                cidx = lax.broadcasted_iota(jnp.int32, s.shape, 1)
                s = jnp.where(cidx <= r, s, NEG)
            m_prev = m_ref[hh]
            m_new = jnp.maximum(m_prev, jnp.max(s, axis=1, keepdims=True))
            p = jnp.exp(s - m_new)
            alpha = jnp.exp(m_prev - m_new)
            l_ref[hh] = alpha * l_ref[hh] + jnp.sum(p, axis=1, keepdims=True)
            acc_ref[hh] = alpha * acc_ref[hh] + _dot(p.astype(BF16), vb)
            m_ref[hh] = m_new

    def needed(j):
        k_last = pl.multiple_of(j * tq, tq) + tq - LANES
        c_last = c_ref[0, 0, :, pl.ds(k_last, LANES)][:, LANES - 1:]
        best = s_bound - (c_last - c_base) + 1e-5 * jnp.abs(c_last)
        worst = NEG
        for hh in range(2):
            gap = best[hh:hh + 1, :] - jnp.min(m_ref[hh])
            worst = jnp.maximum(worst, jnp.max(gap))
        return worst > -SKIP_THRESH

    step(i, True)

    def wanted(j):
        return jnp.logical_and(j >= 0, needed(jnp.maximum(j, 0))).astype(jnp.int32)

    def body(state):
        j, _ = state
        step(j, False)
        return j - 1, wanted(j - 1)

    lax.while_loop(lambda state: state[1] > 0, body, (i - 1, wanted(i - 1)))

    out = jnp.where(first, acc_ref[0] / l_ref[0], acc_ref[1] / l_ref[1])
    o_ref[0] = out.astype(o_ref.dtype)


def _attn_prompt(qb, kb, vb, ct, s_bound, tq):
    b, t, _ = qb.shape
    pairs = N_HEADS // 2
    c4 = ct.reshape(b, pairs, 2, t)
    grid_spec = pltpu.PrefetchScalarGridSpec(
        num_scalar_prefetch=1,
        grid=(b, pairs, t // tq),
        in_specs=[
            pl.BlockSpec((1, tq, LANES), lambda bi, hp, i, sb: (bi, i, hp)),
            pl.BlockSpec((1, t, LANES), lambda bi, hp, i, sb: (bi, 0, hp)),
            pl.BlockSpec((1, t, LANES), lambda bi, hp, i, sb: (bi, 0, hp)),
            pl.BlockSpec((1, 1, 2, t), lambda bi, hp, i, sb: (bi, hp, 0, 0)),
        ],
        out_specs=pl.BlockSpec((1, tq, LANES), lambda bi, hp, i, sb: (bi, i, hp)),
        scratch_shapes=[pltpu.VMEM((2, tq, 1), F32), pltpu.VMEM((2, tq, 1), F32),
                        pltpu.VMEM((2, tq, LANES), F32)],
    )
    return pl.pallas_call(
        _attn_kernel,
        name="attn_prompt",
        grid_spec=grid_spec,
        out_shape=jax.ShapeDtypeStruct((b, t, ATT_W), BF16),
        compiler_params=_params(("arbitrary", "arbitrary", "arbitrary")),
    )(s_bound, qb, kb, vb, c4)


def _ln_silu(y, g, beta):
    mu = jnp.mean(y, axis=-1, keepdims=True)
    yc = y - mu
    var = jnp.mean(yc * yc, axis=-1, keepdims=True)
    return _silu(yc * lax.rsqrt(var + EPS) * g + beta)


def _conv_kernel(prev_ref, cur_ref, w_ref, b_ref, g_ref, beta_ref, o_ref, ext_ref):
    i = pl.program_id(1)
    tm = cur_ref.shape[1]
    halo = prev_ref[0]
    ext_ref[0, 0:CONV_HALO, :] = jnp.where(i == 0, jnp.zeros_like(halo), halo)
    ext_ref[0, CONV_HALO:, :] = cur_ref[0]
    span = tm + CONV_HALO - SUBLANES
    for s in range(1, SUBLANES):
        ext_ref[s, 0:span, :] = ext_ref[0, s:s + span, :]
    first = CONV_HALO - (CONV_W - 1)

    def chunk(ci, carry):
        r0 = pl.multiple_of(ci * CONV_CHUNK, CONV_CHUNK)
        acc = jnp.zeros((CONV_CHUNK, cur_ref.shape[2]), F32) + b_ref[...]
        for j in range(CONV_W):
            whole, s = divmod(first + j, SUBLANES)
            w_rows = jnp.concatenate([w_ref[j]] * (CONV_CHUNK // SUBLANES), axis=0)
            acc = acc + ext_ref[s, pl.ds(r0 + SUBLANES * whole, CONV_CHUNK), :] * w_rows
        o_ref[0, pl.ds(r0, CONV_CHUNK), :] = _ln_silu(acc, g_ref[...], beta_ref[...]).astype(o_ref.dtype)
        return carry

    lax.fori_loop(0, tm // CONV_CHUNK, chunk, 0)


def _conv_prompt(u, w, bias, g, beta, tm):
    b, t, c = u.shape
    per = tm // CONV_HALO
    const = lambda shape: pl.BlockSpec(shape, lambda bi, i: (0,) * len(shape))
    w_rows = jnp.broadcast_to(w[:, None, :], (CONV_W, SUBLANES, c))
    return pl.pallas_call(
        _conv_kernel,
        name="conv_prompt",
        grid=(b, t // tm),
        in_specs=[
            pl.BlockSpec((1, CONV_HALO, c), lambda bi, i: (bi, jnp.maximum(i * per - 1, 0), 0)),
            pl.BlockSpec((1, tm, c), lambda bi, i: (bi, i, 0)),
            const((CONV_W, SUBLANES, c)), const((1, c)), const((1, c)), const((1, c)),
        ],
        out_specs=pl.BlockSpec((1, tm, c), lambda bi, i: (bi, i, 0)),
        out_shape=jax.ShapeDtypeStruct((b, t, c), BF16),
        scratch_shapes=[pltpu.VMEM((SUBLANES, tm + CONV_HALO, c), F32)],
        compiler_params=_params(("arbitrary", "arbitrary")),
    )(u, u, w_rows, bias.reshape(1, c), g.reshape(1, c), beta.reshape(1, c))


def _conv_sample_kernel(state_ref, u_ref, w_ref, b_ref, g_ref, beta_ref, o_ref, ext_ref):
    hist = state_ref.shape[1]
    s = u_ref.shape[1]
    ext_ref[:, 0:hist, :] = state_ref[...]
    ext_ref[:, hist:hist + s, :] = u_ref[...]
    acc = jnp.zeros(u_ref.shape, F32) + b_ref[...]
    for j in range(CONV_W):
        acc = acc + ext_ref[:, j:j + s, :] * w_ref[j:j + 1, :]
    o_ref[...] = _ln_silu(acc, g_ref[...], beta_ref[...])


def _conv_sample(state, u, w, bias, g, beta):
    b, s, c = u.shape
    return pl.pallas_call(
        _conv_sample_kernel,
        name="conv_sample",
        out_shape=jax.ShapeDtypeStruct((b, s, c), F32),
        scratch_shapes=[pltpu.VMEM((b, state.shape[1] + s, c), F32)],
        compiler_params=pltpu.CompilerParams(vmem_limit_bytes=VMEM_LIMIT),
    )(state, u, w, bias.reshape(1, c), g.reshape(1, c), beta.reshape(1, c))


def _mixout_kernel(x_ref, a_ref, c_ref, wa_ref, wc_ref, o_ref):
    o_ref[...] = x_ref[...] + _dot(a_ref[...], wa_ref[...]) + _dot(c_ref[...], wc_ref[...])


def _mixout(x, attn, conv, wa, wc, tm):
    m, d = x.shape
    const = lambda shape: pl.BlockSpec(shape, lambda i: (0,) * len(shape))
    row = lambda w: pl.BlockSpec((tm, w), lambda i: (i, 0))
    return pl.pallas_call(
        _mixout_kernel,
        name="mixout",
        grid=(m // tm,),
        in_specs=[row(d), row(attn.shape[1]), row(conv.shape[1]), const(wa.shape), const(wc.shape)],
        out_specs=row(d),
        out_shape=jax.ShapeDtypeStruct((m, d), F32),
        compiler_params=_params(("arbitrary",)),
    )(x, attn, conv, wa, wc)


def _ple_kernel(x_ref, p_ref, g_ref, wg_ref, wp_ref, o_ref):
    x = x_ref[...]
    gate = _sigmoid(_dot(_rms(x, g_ref[...]).astype(BF16), wg_ref[...]))
    o_ref[...] = x + gate * _dot(p_ref[...].astype(BF16), wp_ref[...])


def _ple(x, p, g, wg, wp, tm):
    m, d = x.shape
    const = lambda shape: pl.BlockSpec(shape, lambda i: (0,) * len(shape))
    row = lambda w: pl.BlockSpec((tm, w), lambda i: (i, 0))
    return pl.pallas_call(
        _ple_kernel,
        name="ple",
        grid=(m // tm,),
        in_specs=[row(d), row(p.shape[1]), const((1, d)), const(wg.shape), const(wp.shape)],
        out_specs=row(d),
        out_shape=jax.ShapeDtypeStruct((m, d), F32),
        compiler_params=_params(("arbitrary",)),
    )(x, p, g.reshape(1, d), wg, wp)


def _sample_attn_kernel(pt_ref, q_ref, kn_ref, vn_ref, lfn_ref, *rest):
    npg = PAGES_PER_STEP
    k_refs, v_refs, lf_refs = rest[:npg], rest[npg:2 * npg], rest[2 * npg:3 * npg]
    o_ref, qbd_ref, m_ref, l_ref, acc_ref, r_ref = rest[3 * npg:]
    g = pl.program_id(1)
    s_q = q_ref.shape[1]
    rows = s_q * N_HEADS
    page = kn_ref.shape[1]

    row_i = lax.broadcasted_iota(jnp.int32, (rows, ATT_W), 0)
    lane_i = lax.broadcasted_iota(jnp.int32, (rows, ATT_W), 1)
    own_head = (lane_i // HEAD_DIM) == (row_i % N_HEADS)

    jj = lax.broadcasted_iota(jnp.int32, (page, page), 0)
    ss = lax.broadcasted_iota(jnp.int32, (page, page), 1)
    later = jnp.where(jj > ss, 1.0, 0.0).astype(BF16)
    ones = jnp.ones((page, page), BF16)

    def visit(k_page, v_page, lf_page, newest):
        s = _dot_nt(qbd_ref[...], k_page.astype(BF16))
        parts = _split3(lf_page)
        suffix = sum(_dot_tn(p, later) for p in parts)
        total = sum(_dot_tn(p, ones) for p in parts)
        bias = suffix + r_ref[...]
        s = s + jnp.concatenate([bias] * s_q, axis=0)
        if newest:
            key_i = lax.broadcasted_iota(jnp.int32, s.shape, 1)
            q_i = lax.broadcasted_iota(jnp.int32, s.shape, 0) // N_HEADS
            s = jnp.where(key_i <= q_i, s, NEG)
        m_prev = m_ref[...]
        m_new = jnp.maximum(m_prev, jnp.max(s, axis=1, keepdims=True))
        p = jnp.exp(s - m_new)
        alpha = jnp.exp(m_prev - m_new)
        l_ref[...] = alpha * l_ref[...] + jnp.sum(p, axis=1, keepdims=True)
        acc_ref[...] = alpha * acc_ref[...] + _dot(p.astype(BF16), v_page.astype(BF16))
        m_ref[...] = m_new
        r_ref[...] = r_ref[...] + total

    @pl.when(g == 0)
    def _():
        q = q_ref[0]
        q_rep = jnp.broadcast_to(q[:, None, :], (s_q, N_HEADS, ATT_W)).reshape(rows, ATT_W)
        qbd_ref[...] = jnp.where(own_head, q_rep, 0.0).astype(BF16)
        m_ref[...] = jnp.full_like(m_ref, NEG)
        l_ref[...] = jnp.zeros_like(l_ref)
        acc_ref[...] = jnp.zeros_like(acc_ref)
        r_ref[...] = jnp.zeros_like(r_ref)
        visit(kn_ref[0], vn_ref[0], lfn_ref[0], True)

    for pi in range(npg):
        visit(k_refs[pi][0], v_refs[pi][0], lf_refs[pi][0], False)

    @pl.when(g == pl.num_programs(1) - 1)
    def _():
        out = jnp.where(own_head, acc_ref[...] / l_ref[...], 0.0)
        o_ref[0] = jnp.sum(out.reshape(s_q, N_HEADS, ATT_W), axis=1)


def _sample_attn(pt, q, k_new, v_new, lf_new, cache_k, cache_v, cache_lf, n_pages):
    b, s_q, _ = q.shape
    page = cache_k.shape[1]
    npg = PAGES_PER_STEP
    steps = n_pages // npg

    def per_batch(shape):
        return pl.BlockSpec((1,) + shape, lambda bi, g, pt_ref: (bi, 0, 0))

    def cached(width, pi):
        def index(bi, g, pt_ref):
            return (pt_ref[bi * n_pages + (n_pages - 1 - g * npg - pi)], 0, 0)
        return pl.BlockSpec((1, page, width), index)

    in_specs = [per_batch((s_q, ATT_W)), per_batch((page, ATT_W)), per_batch((page, ATT_W)),
                per_batch((page, N_HEADS))]
    in_specs += [cached(ATT_W, pi) for pi in range(npg)]
    in_specs += [cached(ATT_W, pi) for pi in range(npg)]
    in_specs += [cached(N_HEADS, pi) for pi in range(npg)]
    rows = s_q * N_HEADS
    grid_spec = pltpu.PrefetchScalarGridSpec(
        num_scalar_prefetch=1,
        grid=(b, steps),
        in_specs=in_specs,
        out_specs=per_batch((s_q, ATT_W)),
        scratch_shapes=[pltpu.VMEM((rows, ATT_W), BF16), pltpu.VMEM((rows, 1), F32),
                        pltpu.VMEM((rows, 1), F32), pltpu.VMEM((rows, ATT_W), F32),
                        pltpu.VMEM((N_HEADS, page), F32)],
    )
    return pl.pallas_call(
        _sample_attn_kernel,
        name="attn_sample",
        grid_spec=grid_spec,
        out_shape=jax.ShapeDtypeStruct((b, s_q, ATT_W), F32),
        compiler_params=_params(("arbitrary", "arbitrary")),
    )(pt, q, k_new, v_new, lf_new, *([cache_k] * npg), *([cache_v] * npg), *([cache_lf] * npg))


def _layer_weights(l, w):
    d = w["w_mix_in"].shape[1]
    wmi = w["w_mix_in"][l]
    n_qkv = 3 * ATT_W
    wf = jnp.zeros((d, LANES), F32).at[:, :N_HEADS].set(wmi[:, n_qkv:n_qkv + N_HEADS])
    bfp = jnp.zeros((1, LANES), F32).at[0, :N_HEADS].set(w["b_forget"][l])
    head_of = jnp.arange(ATT_W) // HEAD_DIM
    gmat = jnp.where(head_of[:, None] == head_of[None, :], 1.0 / HEAD_DIM, 0.0).astype(BF16)
    s_bound = (HEAD_DIM ** 0.5) * jnp.max(jnp.abs(w["q_norm"][l])) * jnp.max(jnp.abs(w["k_norm"][l])) * 1.02
    wmo = w["w_mix_out"][l].astype(BF16)
    return dict(
        ffn1_norm=w["ffn1_norm"][l], ffn1_w_in=w["ffn1_w_in_bf"][l], ffn1_w_out=w["ffn1_w_out_bf"][l],
        mix_norm=w["mix_norm"][l],
        wqkv=wmi[:, :n_qkv].astype(BF16), wf=wf.astype(BF16), wc=wmi[:, n_qkv + N_HEADS:].astype(BF16),
        bfp=bfp, qg=jnp.tile(w["q_norm"][l], N_HEADS).reshape(1, ATT_W),
        kg=jnp.tile(w["k_norm"][l], N_HEADS).reshape(1, ATT_W), gmat=gmat,
        s_bound=s_bound.reshape(1).astype(F32),
        conv_w=w["conv_w"][l], conv_b=w["conv_b"][l], conv_ln_g=w["conv_ln_g"][l],
        conv_ln_b=w["conv_ln_b"][l], wmo_a=wmo[:ATT_W], wmo_c=wmo[ATT_W:],
        ffn2_norm=w["ffn2_norm"][l], ffn2_w_in=w["ffn2_w_in_bf"][l], ffn2_w_out=w["ffn2_w_out_bf"][l],
        ple_norm=w["ple_norm"][l], wg=w["w_ple_gate"][l].astype(BF16), wp=w["w_ple_proj"][l].astype(BF16),
    )


def _mixin_call(x3, lw, tm):
    return _mixin(x3, lw["mix_norm"], lw["wqkv"], lw["wf"], lw["wc"], lw["bfp"], lw["qg"], lw["kg"],
                  lw["gmat"], tm)


def _finish_layer(x, attn, conv, p, lw, tm):
    x = _mixout(x, attn, conv, lw["wmo_a"], lw["wmo_c"], tm)
    x = _ffn(x, lw["ffn2_norm"], lw["ffn2_w_in"], lw["ffn2_w_out"], tm)
    return _ple(x, p, lw["ple_norm"], lw["wg"], lw["wp"], tm)


def kernel(x_prompt, x_sample, p_prompt, p_sample, cache_k, cache_v, cache_logf, state_conv, page_table, ffn1_norm, ffn1_w_in, ffn1_w_out, mix_norm, w_mix_in, b_forget, q_norm, k_norm, conv_w, conv_b, conv_ln_g, conv_ln_b, w_mix_out, ffn2_norm, ffn2_w_in, ffn2_w_out, ple_norm, w_ple_gate, w_ple_proj):
    depth = ffn1_norm.shape[0]
    b, t, d = x_prompt.shape
    sb, ss, _ = x_sample.shape
    n_pool, page = cache_k.shape[1], cache_k.shape[2]
    n_pages = page_table.shape[1]
    hist = state_conv.shape[2]
    ms = sb * ss

    w = dict(ffn1_norm=ffn1_norm, mix_norm=mix_norm, w_mix_in=w_mix_in, b_forget=b_forget, q_norm=q_norm,
             k_norm=k_norm, conv_w=conv_w, conv_b=conv_b, conv_ln_g=conv_ln_g, conv_ln_b=conv_ln_b,
             w_mix_out=w_mix_out, ffn2_norm=ffn2_norm, ple_norm=ple_norm, w_ple_gate=w_ple_gate,
             w_ple_proj=w_ple_proj,
             ffn1_w_in_bf=ffn1_w_in.astype(BF16), ffn1_w_out_bf=ffn1_w_out.astype(BF16),
             ffn2_w_in_bf=ffn2_w_in.astype(BF16), ffn2_w_out_bf=ffn2_w_out.astype(BF16))

    ck = cache_k.reshape(depth * n_pool, page, ATT_W)
    cv = cache_v.reshape(depth * n_pool, page, ATT_W)
    clf = cache_logf.reshape(depth * n_pool, page, N_HEADS)

    xp = x_prompt.reshape(b * t, d)
    xs = x_sample.reshape(ms, d)
    outs = {name: [] for name in ("kp", "vp", "fp", "cp", "ks", "vs", "fs", "cs")}

    for l in range(depth):
        lw = _layer_weights(l, w)

        xp = _ffn(xp, lw["ffn1_norm"], lw["ffn1_w_in"], lw["ffn1_w_out"], ROW_TILE)
        qb, kb, vb, k, v, lf, ct, u = _mixin_call(xp.reshape(b, t, d), lw, ROW_TILE)
        attn = _attn_prompt(qb, kb, vb, ct, lw["s_bound"], ATT_TILE)
        conv = _conv_prompt(u, lw["conv_w"], lw["conv_b"], lw["conv_ln_g"], lw["conv_ln_b"], ROW_TILE)
        xp = _finish_layer(xp, attn.reshape(b * t, ATT_W), conv.reshape(b * t, -1),
                           p_prompt[l].reshape(b * t, -1), lw, ROW_TILE)
        outs["kp"].append(k.reshape(b, t, N_HEADS, HEAD_DIM))
        outs["vp"].append(v.reshape(b, t, N_HEADS, HEAD_DIM))
        outs["fp"].append(lf)
        outs["cp"].append(u[:, t - hist:, :])

        xs = _ffn(xs, lw["ffn1_norm"], lw["ffn1_w_in"], lw["ffn1_w_out"], ms)
        qb, _, _, k, v, lf, _, u = _mixin_call(xs.reshape(1, ms, d), lw, ms)
        pad = lambda a: jnp.pad(a.reshape(sb, ss, -1), ((0, 0), (0, page - ss), (0, 0)))
        pt = (page_table + l * n_pool).reshape(-1)
        attn = _sample_attn(pt, qb.reshape(sb, ss, ATT_W).astype(F32), pad(k), pad(v), pad(lf),
                            ck, cv, clf, n_pages)
        u3 = u.reshape(sb, ss, -1)
        conv = _conv_sample(state_conv[l], u3, lw["conv_w"], lw["conv_b"], lw["conv_ln_g"], lw["conv_ln_b"])
        xs = _finish_layer(xs, attn.reshape(ms, ATT_W).astype(BF16), conv.reshape(ms, -1).astype(BF16),
                           p_sample[l].reshape(ms, -1), lw, ms)
        outs["ks"].append(k.reshape(sb, ss, N_HEADS, HEAD_DIM))
        outs["vs"].append(v.reshape(sb, ss, N_HEADS, HEAD_DIM))
        outs["fs"].append(lf.reshape(sb, ss, N_HEADS))
        outs["cs"].append(jnp.concatenate([state_conv[l], u3], axis=1)[:, ss:, :])

    stack = lambda name: jnp.stack(outs[name])
    return (xp.reshape(b, t, d), xs.reshape(sb, ss, d), stack("kp"), stack("vp"), stack("fp"), stack("cp"),
            stack("ks"), stack("vs"), stack("fs"), stack("cs"))
```

```python
import jax
import jax.numpy as jnp
from jax import lax
from jax.experimental import pallas as pl
from jax.experimental.pallas import tpu as pltpu

F32 = jnp.float32
BF16 = jnp.bfloat16

EPS = 1e-6
N_HEADS = 8
HEAD_DIM = 64
ATT_W = N_HEADS * HEAD_DIM
CONV_W = 31
LANES = 128
SUBLANES = 8
BF16_ROWS = 16
V7X_VMEM_BYTES = 64 * 2 ** 20
VMEM_LIMIT = V7X_VMEM_BYTES - 8 * 2 ** 20
NEG = -1e30
SKIP_THRESH = 106.0
FIXED_REF_LIMIT = 60.0

ROW_TILE = 512
ATT_TILE = 512
CONV_HALO = 32
CONV_CHUNK = 32
PAGES_PER_STEP = 8


def _params(sem):
    return pltpu.CompilerParams(dimension_semantics=sem, vmem_limit_bytes=VMEM_LIMIT)


def _rms(x, g):
    return x * lax.rsqrt(jnp.mean(x * x, axis=-1, keepdims=True) + EPS) * g


def _sigmoid(x):
    return 1.0 / (1.0 + jnp.exp(-x))


def _silu(x):
    return x * _sigmoid(x)


def _log_sigmoid(x):
    return -(jnp.maximum(-x, 0.0) + jnp.log1p(jnp.exp(-jnp.abs(x))))


def _dot(a, b):
    return jnp.dot(a, b, preferred_element_type=F32)


def _dot_nt(a, b):
    return lax.dot_general(a, b, (((1,), (1,)), ((), ())), preferred_element_type=F32)


def _split3(x):
    hi = x.astype(BF16)
    r = x - hi.astype(F32)
    mid = r.astype(BF16)
    lo = (r - mid.astype(F32)).astype(BF16)
    return hi, mid, lo


def _lane_tile(x, width):
    return jnp.concatenate([x] * (width // LANES), axis=1)


def _ffn_kernel(x_ref, g_ref, wg_ref, wu_ref, wo_ref, o_ref, h_ref, acc_ref):
    f = pl.program_id(1)

    @pl.when(f == 0)
    def _():
        h_ref[...] = _rms(x_ref[...], g_ref[...]).astype(BF16)

    h = h_ref[...]
    gate = _dot(h, wg_ref[...])
    up = _dot(h, wu_ref[...])
    part = _dot((_silu(gate) * up).astype(BF16), wo_ref[...])

    @pl.when(f == 0)
    def _():
        acc_ref[...] = part

    @pl.when(f > 0)
    def _():
        acc_ref[...] += part

    @pl.when(f == pl.num_programs(1) - 1)
    def _():
        o_ref[...] = x_ref[...] + 0.5 * acc_ref[...]


def _ffn(x, g, w_in, w_out, tm):
    m, d = x.shape
    dff = w_out.shape[0]
    tf = dff // 2 if (dff // 2) % LANES == 0 else dff
    nf = dff // tf
    return pl.pallas_call(
        _ffn_kernel,
        name="ffn",
        grid=(m // tm, nf),
        in_specs=[
            pl.BlockSpec((tm, d), lambda i, f: (i, 0)),
            pl.BlockSpec((1, d), lambda i, f: (0, 0)),
            pl.BlockSpec((d, tf), lambda i, f: (0, f)),
            pl.BlockSpec((d, tf), lambda i, f: (0, f + nf)),
            pl.BlockSpec((tf, d), lambda i, f: (f, 0)),
        ],
        out_specs=pl.BlockSpec((tm, d), lambda i, f: (i, 0)),
        out_shape=jax.ShapeDtypeStruct((m, d), F32),
        scratch_shapes=[pltpu.VMEM((tm, d), BF16), pltpu.VMEM((tm, d), F32)],
        compiler_params=_params(("arbitrary", "arbitrary")),
    )(x, g.reshape(1, d), w_in, w_in, w_out)


def _lane_cumsum(x):
    n = x.shape[1]
    idx = lax.broadcasted_iota(jnp.int32, x.shape, 1)
    s = 1
    while s < n:
        x = x + jnp.where(idx >= s, pltpu.roll(x, s, axis=1), 0.0)
        s *= 2
    return x


def _mixin_kernel(x_ref, g_ref, wt_ref, wn_ref, bf_ref, qg_ref, kgt_ref, gmat_ref,
                  qb_ref, ktb_ref, vb_ref, kt_ref, vt_ref, lft_ref, ct_ref, u_ref, carry_ref):
    i = pl.program_id(1)
    tm = x_ref.shape[1]
    h = _rms(x_ref[0], g_ref[...]).astype(BF16)

    zt = _dot_nt(wt_ref[...], h)
    k3 = zt[:ATT_W].reshape(N_HEADS, HEAD_DIM, tm)
    ms = jnp.mean(k3 * k3, axis=1, keepdims=True)
    ktn = (k3 * lax.rsqrt(ms + EPS)).reshape(ATT_W, tm) * _lane_tile(kgt_ref[...], tm)
    kt_ref[0] = ktn
    ktb_ref[0] = ktn.astype(BF16)
    vt = zt[ATT_W:2 * ATT_W]
    vt_ref[0] = vt
    vb_ref[0] = vt.T.astype(BF16)

    lft = _log_sigmoid(zt[2 * ATT_W:2 * ATT_W + N_HEADS] + _lane_tile(bf_ref[...], tm))
    lft_ref[0] = lft

    @pl.when(i == 0)
    def _():
        carry_ref[...] = jnp.zeros_like(carry_ref)

    c = _lane_cumsum(lft) + carry_ref[:, :1]
    ct_ref[0] = c
    carry_ref[...] = jnp.broadcast_to(c[:, -1:], carry_ref.shape)

    z = _dot(h, wn_ref[...])
    q = z[:, :ATT_W]
    ms_q = _dot((q * q).astype(BF16), gmat_ref[...])
    qb_ref[0] = (q * lax.rsqrt(ms_q + EPS) * qg_ref[...] * (HEAD_DIM ** -0.5)).astype(BF16)
    cw = (z.shape[1] - ATT_W) // 2
    u_ref[0] = z[:, ATT_W:ATT_W + cw] * _sigmoid(z[:, ATT_W + cw:])


def _mixin(x, g, wt, wn, bfp, qg, kgt, gmat, tm):
    b, t, d = x.shape
    cw = (wn.shape[1] - ATT_W) // 2
    const = lambda shape: pl.BlockSpec(shape, lambda bi, i: (0,) * len(shape))
    row = lambda w: pl.BlockSpec((1, tm, w), lambda bi, i: (bi, i, 0))
    col = lambda r: pl.BlockSpec((1, r, tm), lambda bi, i: (bi, 0, i))
    return pl.pallas_call(
        _mixin_kernel,
        name="mixin",
        grid=(b, t // tm),
        in_specs=[row(d), const((1, d)), const(wt.shape), const(wn.shape), const(bfp.shape),
                  const(qg.shape), const(kgt.shape), const(gmat.shape)],
        out_specs=[row(ATT_W), col(ATT_W), row(ATT_W), col(ATT_W), col(ATT_W), col(N_HEADS),
                   col(N_HEADS), row(cw)],
        out_shape=[jax.ShapeDtypeStruct((b, t, ATT_W), BF16), jax.ShapeDtypeStruct((b, ATT_W, t), BF16),
                   jax.ShapeDtypeStruct((b, t, ATT_W), BF16), jax.ShapeDtypeStruct((b, ATT_W, t), F32),
                   jax.ShapeDtypeStruct((b, ATT_W, t), F32), jax.ShapeDtypeStruct((b, N_HEADS, t), F32),
                   jax.ShapeDtypeStruct((b, N_HEADS, t), F32), jax.ShapeDtypeStruct((b, t, cw), F32)],
        scratch_shapes=[pltpu.VMEM((N_HEADS, LANES), F32)],
        compiler_params=_params(("arbitrary", "arbitrary")),
    )(x, g.reshape(1, d), wt, wn, bfp, qg, kgt, gmat)


def _head_pair(q_ref):
    q = q_ref[0]
    first = lax.broadcasted_iota(jnp.int32, q.shape, 1) < HEAD_DIM
    zero = jnp.zeros_like(q)
    return (jnp.where(first, q, zero), jnp.where(first, zero, q)), first


def _attn_fixed_kernel(sb_ref, q_ref, kt_ref, v_ref, c2_ref, c8_ref, o_ref, acc_ref):
    hp = pl.program_id(1)
    i = pl.program_id(2)
    tq = q_ref.shape[1]
    t = kt_ref.shape[2]
    q_heads, first = _head_pair(q_ref)
    q0 = pl.multiple_of(i * tq, tq)
    c_base = c2_ref[0, 0, :, pl.ds(q0, LANES)][:, :1]

    slab = jnp.concatenate([c8_ref[0, :, pl.ds(q0, tq)], jnp.zeros((LANES - N_HEADS, tq), F32)], axis=0)
    by_row = slab.T
    head_lane = lax.broadcasted_iota(jnp.int32, by_row.shape, 1)
    shift = []
    for hh in range(2):
        mine = jnp.sum(jnp.where(head_lane == 2 * hp + hh, by_row, 0.0), axis=1, keepdims=True)
        shift.append(mine - c_base[hh:hh + 1, :] - sb_ref[0])

    c_all = c2_ref[0, 0]
    col = lax.broadcasted_iota(jnp.int32, c_all.shape, 1)
    block_end = jnp.logical_and(col % tq == tq - 1, col < q0)
    alive = (c_base - c_all) + 1e-5 * jnp.abs(c_all) >= -SKIP_THRESH
    oldest = jnp.min(jnp.where(jnp.logical_and(block_end, alive), col // tq, i))
    del t

    acc_ref[...] = jnp.zeros_like(acc_ref)
    one = jnp.ones((tq, LANES), BF16)

    def step(j, diagonal):
        k0 = pl.multiple_of(j * tq, tq)
        kt = kt_ref[0, :, pl.ds(k0, tq)]
        vb = v_ref[0, pl.ds(k0, tq), :]
        ck = c2_ref[0, 0, :, pl.ds(k0, tq)] - c_base
        first_k = lax.broadcasted_iota(jnp.int32, vb.shape, 1) < HEAD_DIM
        v_aug = (jnp.where(first_k, vb, one), jnp.where(first_k, one, vb))
        for hh in range(2):
            x = (_dot(q_heads[hh], kt) + shift[hh]) - ck[hh:hh + 1, :]
            if diagonal:
                r = lax.broadcasted_iota(jnp.int32, x.shape, 0)
                cidx = lax.broadcasted_iota(jnp.int32, x.shape, 1)
                x = jnp.where(cidx <= r, x, NEG)
            acc_ref[hh] += _dot(jnp.exp(x).astype(BF16), v_aug[hh])

    step(i, True)

    def body(n, carry):
        step(i - 1 - n, False)
        return carry

    lax.fori_loop(0, i - oldest, body, 0)

    a0, a1 = acc_ref[0], acc_ref[1]
    out = jnp.where(first, a0 / pltpu.roll(a0, HEAD_DIM, axis=1), a1 / pltpu.roll(a1, HEAD_DIM, axis=1))
    o_ref[0] = out.astype(o_ref.dtype)


def _attn_online_kernel(sb_ref, q_ref, kt_ref, v_ref, c2_ref, c8_ref, o_ref, m_ref, l_ref, acc_ref):
    del c8_ref
    i = pl.program_id(2)
    tq = q_ref.shape[1]
    q_heads, first = _head_pair(q_ref)
    q0 = pl.multiple_of(i * tq, tq)
    c_base = c2_ref[0, 0, :, pl.ds(q0, LANES)][:, :1]
    s_bound = sb_ref[0]

    m_ref[...] = jnp.full_like(m_ref, NEG)
    l_ref[...] = jnp.zeros_like(l_ref)
    acc_ref[...] = jnp.zeros_like(acc_ref)

    def step(j, diagonal):
        k0 = pl.multiple_of(j * tq, tq)
        kt = kt_ref[0, :, pl.ds(k0, tq)]
        vb = v_ref[0, pl.ds(k0, tq), :]
        ck = c2_ref[0, 0, :, pl.ds(k0, tq)] - c_base
        for hh in range(2):
            s = _dot(q_heads[hh], kt) - ck[hh:hh + 1, :]
            if diagonal:
                r = lax.broadcasted_iota(jnp.int32, s.shape, 0)
                cidx = lax.broadcasted_iota(jnp.int32, s.shape, 1)
                s = jnp.where(cidx <= r, s, NEG)
            m_prev = m_ref[hh]
            m_new = jnp.maximum(m_prev, jnp.max(s, axis=1, keepdims=True))
            p = jnp.exp(s - m_new)
            alpha = jnp.exp(m_prev - m_new)
            l_ref[hh] = alpha * l_ref[hh] + jnp.sum(p, axis=1, keepdims=True)
            acc_ref[hh] = alpha * acc_ref[hh] + _dot(p.astype(BF16), vb)
            m_ref[hh] = m_new

    def needed(j):
        k_last = pl.multiple_of(j * tq, tq) + tq - LANES
        c_last = c2_ref[0, 0, :, pl.ds(k_last, LANES)][:, LANES - 1:]
        best = s_bound - (c_last - c_base) + 1e-5 * jnp.abs(c_last)
        worst = NEG
        for hh in range(2):
            worst = jnp.maximum(worst, jnp.max(best[hh:hh + 1, :] - jnp.min(m_ref[hh])))
        return worst > -SKIP_THRESH

    step(i, True)

    def wanted(j):
        return jnp.logical_and(j >= 0, needed(jnp.maximum(j, 0))).astype(jnp.int32)

    def body(state):
        j, _ = state
        step(j, False)
        return j - 1, wanted(j - 1)

    lax.while_loop(lambda state: state[1] > 0, body, (i - 1, wanted(i - 1)))

    out = jnp.where(first, acc_ref[0] / l_ref[0], acc_ref[1] / l_ref[1])
    o_ref[0] = out.astype(o_ref.dtype)


def _attn_prompt(qb, ktb, vb, ct, s_bound, tq):
    b, t, _ = qb.shape
    pairs = N_HEADS // 2
    c4 = ct.reshape(b, pairs, 2, t)

    def call(body, name, scratch):
        grid_spec = pltpu.PrefetchScalarGridSpec(
            num_scalar_prefetch=1,
            grid=(b, pairs, t // tq),
            in_specs=[
                pl.BlockSpec((1, tq, LANES), lambda bi, hp, i, sb: (bi, i, hp)),
                pl.BlockSpec((1, LANES, t), lambda bi, hp, i, sb: (bi, hp, 0)),
                pl.BlockSpec((1, t, LANES), lambda bi, hp, i, sb: (bi, 0, hp)),
                pl.BlockSpec((1, 1, 2, t), lambda bi, hp, i, sb: (bi, hp, 0, 0)),
                pl.BlockSpec((1, N_HEADS, t), lambda bi, hp, i, sb: (bi, 0, 0)),
            ],
            out_specs=pl.BlockSpec((1, tq, LANES), lambda bi, hp, i, sb: (bi, i, hp)),
            scratch_shapes=scratch,
        )
        return pl.pallas_call(
            body,
            name=name,
            grid_spec=grid_spec,
            out_shape=jax.ShapeDtypeStruct((b, t, ATT_W), BF16),
            compiler_params=_params(("arbitrary", "arbitrary", "arbitrary")),
        )(s_bound, qb, ktb, vb, c4, ct)

    acc = pltpu.VMEM((2, tq, LANES), F32)
    stat = pltpu.VMEM((2, tq, 1), F32)
    return lax.cond(
        2.0 * s_bound[0] <= FIXED_REF_LIMIT,
        lambda: call(_attn_fixed_kernel, "attn_prompt", [acc]),
        lambda: call(_attn_online_kernel, "attn_prompt_online", [stat, stat, acc]),
    )


def _ln_silu(y, g, beta):
    mu = jnp.mean(y, axis=-1, keepdims=True)
    yc = y - mu
    var = jnp.mean(yc * yc, axis=-1, keepdims=True)
    return _silu(yc * lax.rsqrt(var + EPS) * g + beta)


def _conv_kernel(prev_ref, cur_ref, w_ref, b_ref, g_ref, beta_ref, o_ref, ext_ref):
    i = pl.program_id(1)
    tm = cur_ref.shape[1]
    halo = prev_ref[0]
    ext_ref[0, 0:CONV_HALO, :] = jnp.where(i == 0, jnp.zeros_like(halo), halo)
    ext_ref[0, CONV_HALO:, :] = cur_ref[0]
    span = tm + CONV_HALO - SUBLANES
    for s in range(1, SUBLANES):
        ext_ref[s, 0:span, :] = ext_ref[0, s:s + span, :]
    first = CONV_HALO - (CONV_W - 1)

    def chunk(ci, carry):
        r0 = pl.multiple_of(ci * CONV_CHUNK, CONV_CHUNK)
        acc = jnp.zeros((CONV_CHUNK, cur_ref.shape[2]), F32) + b_ref[...]
        for j in range(CONV_W):
            whole, s = divmod(first + j, SUBLANES)
            w_rows = jnp.concatenate([w_ref[j]] * (CONV_CHUNK // SUBLANES), axis=0)
            acc = acc + ext_ref[s, pl.ds(r0 + SUBLANES * whole, CONV_CHUNK), :] * w_rows
        o_ref[0, pl.ds(r0, CONV_CHUNK), :] = _ln_silu(acc, g_ref[...], beta_ref[...]).astype(o_ref.dtype)
        return carry

    lax.fori_loop(0, tm // CONV_CHUNK, chunk, 0)


def _conv_prompt(u, w, bias, g, beta, tm):
    b, t, c = u.shape
    per = tm // CONV_HALO
    const = lambda shape: pl.BlockSpec(shape, lambda bi, i: (0,) * len(shape))
    w_rows = jnp.broadcast_to(w[:, None, :], (CONV_W, SUBLANES, c))
    return pl.pallas_call(
        _conv_kernel,
        name="conv_prompt",
        grid=(b, t // tm),
        in_specs=[
            pl.BlockSpec((1, CONV_HALO, c), lambda bi, i: (bi, jnp.maximum(i * per - 1, 0), 0)),
            pl.BlockSpec((1, tm, c), lambda bi, i: (bi, i, 0)),
            const((CONV_W, SUBLANES, c)), const((1, c)), const((1, c)), const((1, c)),
        ],
        out_specs=pl.BlockSpec((1, tm, c), lambda bi, i: (bi, i, 0)),
        out_shape=jax.ShapeDtypeStruct((b, t, c), BF16),
        scratch_shapes=[pltpu.VMEM((SUBLANES, tm + CONV_HALO, c), F32)],
        compiler_params=_params(("arbitrary", "arbitrary")),
    )(u, u, w_rows, bias.reshape(1, c), g.reshape(1, c), beta.reshape(1, c))


def _conv_sample_kernel(state_ref, u_ref, w_ref, b_ref, g_ref, beta_ref, o_ref):
    hist = state_ref.shape[0]
    for t in range(u_ref.shape[0]):
        acc = jnp.zeros(u_ref.shape[1:], F32) + b_ref[...]
        for j in range(CONV_W):
            src = t + j
            row = state_ref[src] if src < hist else u_ref[src - hist]
            acc = acc + row * w_ref[j:j + 1, :]
        o_ref[t] = _ln_silu(acc, g_ref[...], beta_ref[...])


def _conv_sample(state, u, w, bias, g, beta):
    s, b, c = u.shape
    return pl.pallas_call(
        _conv_sample_kernel,
        name="conv_sample",
        out_shape=jax.ShapeDtypeStruct((s, b, c), F32),
        compiler_params=pltpu.CompilerParams(vmem_limit_bytes=VMEM_LIMIT),
    )(state, u, w, bias.reshape(1, c), g.reshape(1, c), beta.reshape(1, c))


def _mixout_kernel(x_ref, a_ref, c_ref, wa_ref, wc_ref, o_ref):
    o_ref[...] = x_ref[...] + _dot(a_ref[...], wa_ref[...]) + _dot(c_ref[...], wc_ref[...])


def _mixout(x, attn, conv, wa, wc, tm):
    m, d = x.shape
    const = lambda shape: pl.BlockSpec(shape, lambda i: (0,) * len(shape))
    row = lambda w: pl.BlockSpec((tm, w), lambda i: (i, 0))
    return pl.pallas_call(
        _mixout_kernel,
        name="mixout",
        grid=(m // tm,),
        in_specs=[row(d), row(attn.shape[1]), row(conv.shape[1]), const(wa.shape), const(wc.shape)],
        out_specs=row(d),
        out_shape=jax.ShapeDtypeStruct((m, d), F32),
        compiler_params=_params(("arbitrary",)),
    )(x, attn, conv, wa, wc)


def _ple_kernel(x_ref, p_ref, g_ref, wg_ref, wp_ref, o_ref):
    x = x_ref[...]
    gate = _sigmoid(_dot(_rms(x, g_ref[...]).astype(BF16), wg_ref[...]))
    o_ref[...] = x + gate * _dot(p_ref[...].astype(BF16), wp_ref[...])


def _ple(x, p, g, wg, wp, tm):
    m, d = x.shape
    const = lambda shape: pl.BlockSpec(shape, lambda i: (0,) * len(shape))
    row = lambda w: pl.BlockSpec((tm, w), lambda i: (i, 0))
    return pl.pallas_call(
        _ple_kernel,
        name="ple",
        grid=(m // tm,),
        in_specs=[row(d), row(p.shape[1]), const((1, d)), const(wg.shape), const(wp.shape)],
        out_specs=row(d),
        out_shape=jax.ShapeDtypeStruct((m, d), F32),
        compiler_params=_params(("arbitrary",)),
    )(x, p, g.reshape(1, d), wg, wp)


def _sample_attn_kernel(pt_ref, q_ref, kn_ref, vn_ref, lfn_ref, *rest):
    del pt_ref
    npg = PAGES_PER_STEP
    k_refs, v_refs, lf_refs = rest[:npg], rest[npg:2 * npg], rest[2 * npg:3 * npg]
    o_ref, qbd_ref, m_ref, l_ref, acc_ref, r_ref = rest[3 * npg:]
    g = pl.program_id(1)
    s_q = q_ref.shape[1]
    rows = s_q * N_HEADS
    page = kn_ref.shape[2]

    row_i = lax.broadcasted_iota(jnp.int32, (rows, ATT_W), 0)
    lane_i = lax.broadcasted_iota(jnp.int32, (rows, ATT_W), 1)
    own_head = (lane_i // HEAD_DIM) == (row_i % N_HEADS)

    jj = lax.broadcasted_iota(jnp.int32, (page, page), 0)
    ss = lax.broadcasted_iota(jnp.int32, (page, page), 1)
    later = jnp.where(jj > ss, 1.0, 0.0).astype(BF16)
    ones = jnp.ones((page, page), BF16)

    def visit(kt_page, vt_page, lft_page, newest):
        s = _dot(qbd_ref[...], kt_page.astype(BF16))
        parts = _split3(lft_page)
        suffix = sum(_dot(p, later) for p in parts)
        total = sum(_dot(p, ones) for p in parts)
        s = s + jnp.concatenate([suffix + r_ref[...]] * s_q, axis=0)
        if newest:
            key_i = lax.broadcasted_iota(jnp.int32, s.shape, 1)
            q_i = lax.broadcasted_iota(jnp.int32, s.shape, 0) // N_HEADS
            s = jnp.where(key_i <= q_i, s, NEG)
        m_prev = m_ref[...]
        m_new = jnp.maximum(m_prev, jnp.max(s, axis=1, keepdims=True))
        p = jnp.exp(s - m_new)
        alpha = jnp.exp(m_prev - m_new)
        l_ref[...] = alpha * l_ref[...] + jnp.sum(p, axis=1, keepdims=True)
        acc_ref[...] = alpha * acc_ref[...] + _dot_nt(p.astype(BF16), vt_page.astype(BF16))
        m_ref[...] = m_new
        r_ref[...] = r_ref[...] + total

    @pl.when(g == 0)
    def _():
        q = q_ref[0]
        q_rep = jnp.broadcast_to(q[:, None, :], (s_q, N_HEADS, ATT_W)).reshape(rows, ATT_W)
        qbd_ref[...] = jnp.where(own_head, q_rep, 0.0).astype(BF16)
        m_ref[...] = jnp.full_like(m_ref, NEG)
        l_ref[...] = jnp.zeros_like(l_ref)
        acc_ref[...] = jnp.zeros_like(acc_ref)
        r_ref[...] = jnp.zeros_like(r_ref)
        visit(kn_ref[0], vn_ref[0], lfn_ref[0], True)

    for pi in range(npg):
        visit(k_refs[pi][0], v_refs[pi][0], lf_refs[pi][0], False)

    @pl.when(g == pl.num_programs(1) - 1)
    def _():
        out = jnp.where(own_head, acc_ref[...] / l_ref[...], 0.0)
        o_ref[0] = jnp.sum(out.reshape(s_q, N_HEADS, ATT_W), axis=1)


def _sample_attn(pt, q, kt_new, vt_new, lft_new, cache_kt, cache_vt, cache_lft, n_pages):
    b, s_q, _ = q.shape
    page = cache_kt.shape[2]
    npg = PAGES_PER_STEP
    steps = n_pages // npg

    def per_batch(shape):
        return pl.BlockSpec((1,) + shape, lambda bi, g, pt_ref: (bi, 0, 0))

    def cached(height, pi):
        def index(bi, g, pt_ref):
            return (pt_ref[bi * n_pages + (n_pages - 1 - g * npg - pi)], 0, 0)
        return pl.BlockSpec((1, height, page), index)

    in_specs = [per_batch((s_q, ATT_W)), per_batch((ATT_W, page)), per_batch((ATT_W, page)),
                per_batch((N_HEADS, page))]
    in_specs += [cached(ATT_W, pi) for pi in range(npg)]
    in_specs += [cached(ATT_W, pi) for pi in range(npg)]
    in_specs += [cached(N_HEADS, pi) for pi in range(npg)]
    rows = s_q * N_HEADS
    grid_spec = pltpu.PrefetchScalarGridSpec(
        num_scalar_prefetch=1,
        grid=(b, steps),
        in_specs=in_specs,
        out_specs=per_batch((s_q, ATT_W)),
        scratch_shapes=[pltpu.VMEM((rows, ATT_W), BF16), pltpu.VMEM((rows, 1), F32),
                        pltpu.VMEM((rows, 1), F32), pltpu.VMEM((rows, ATT_W), F32),
                        pltpu.VMEM((N_HEADS, page), F32)],
    )
    return pl.pallas_call(
        _sample_attn_kernel,
        name="attn_sample",
        grid_spec=grid_spec,
        out_shape=jax.ShapeDtypeStruct((b, s_q, ATT_W), F32),
        compiler_params=_params(("arbitrary", "arbitrary")),
    )(pt, q, kt_new, vt_new, lft_new, *([cache_kt] * npg), *([cache_vt] * npg), *([cache_lft] * npg))


def _layer_weights(l, w):
    wmi = w["w_mix_in"][l]
    d = wmi.shape[0]
    n_qkv = 3 * ATT_W
    pad_rows = BF16_ROWS - N_HEADS
    wt = jnp.concatenate([wmi[:, ATT_W:n_qkv + N_HEADS].T, jnp.zeros((pad_rows, d), F32)], axis=0)
    wn = jnp.concatenate([wmi[:, :ATT_W], wmi[:, n_qkv + N_HEADS:]], axis=1)
    bfp = jnp.broadcast_to(w["b_forget"][l][:, None], (N_HEADS, LANES))
    kgt = jnp.broadcast_to(jnp.tile(w["k_norm"][l], N_HEADS)[:, None], (ATT_W, LANES))
    head_of = jnp.arange(ATT_W) // HEAD_DIM
    gmat = jnp.where(head_of[:, None] == head_of[None, :], 1.0 / HEAD_DIM, 0.0).astype(BF16)
    s_bound = (HEAD_DIM ** 0.5) * jnp.max(jnp.abs(w["q_norm"][l])) * jnp.max(jnp.abs(w["k_norm"][l])) * 1.02
    wmo = w["w_mix_out"][l].astype(BF16)
    return dict(
        ffn1_norm=w["ffn1_norm"][l], ffn1_w_in=w["ffn1_w_in_bf"][l], ffn1_w_out=w["ffn1_w_out_bf"][l],
        mix_norm=w["mix_norm"][l], wt=wt.astype(BF16), wn=wn.astype(BF16), bfp=bfp,
        qg=jnp.tile(w["q_norm"][l], N_HEADS).reshape(1, ATT_W), kgt=kgt, gmat=gmat,
        s_bound=s_bound.reshape(1).astype(F32),
        conv_w=w["conv_w"][l], conv_b=w["conv_b"][l], conv_ln_g=w["conv_ln_g"][l],
        conv_ln_b=w["conv_ln_b"][l], wmo_a=wmo[:ATT_W], wmo_c=wmo[ATT_W:],
        ffn2_norm=w["ffn2_norm"][l], ffn2_w_in=w["ffn2_w_in_bf"][l], ffn2_w_out=w["ffn2_w_out_bf"][l],
        ple_norm=w["ple_norm"][l], wg=w["w_ple_gate"][l].astype(BF16), wp=w["w_ple_proj"][l].astype(BF16),
    )


def _mixin_call(x3, lw, tm):
    return _mixin(x3, lw["mix_norm"], lw["wt"], lw["wn"], lw["bfp"], lw["qg"], lw["kgt"], lw["gmat"], tm)


def _finish_layer(x, attn, conv, p, lw, tm):
    x = _mixout(x, attn, conv, lw["wmo_a"], lw["wmo_c"], tm)
    x = _ffn(x, lw["ffn2_norm"], lw["ffn2_w_in"], lw["ffn2_w_out"], tm)
    return _ple(x, p, lw["ple_norm"], lw["wg"], lw["wp"], tm)


def kernel(x_prompt, x_sample, p_prompt, p_sample, cache_k, cache_v, cache_logf, state_conv, page_table, ffn1_norm, ffn1_w_in, ffn1_w_out, mix_norm, w_mix_in, b_forget, q_norm, k_norm, conv_w, conv_b, conv_ln_g, conv_ln_b, w_mix_out, ffn2_norm, ffn2_w_in, ffn2_w_out, ple_norm, w_ple_gate, w_ple_proj):
    depth = ffn1_norm.shape[0]
    b, t, d = x_prompt.shape
    sb, ss, _ = x_sample.shape
    n_pool, page = cache_k.shape[1], cache_k.shape[2]
    n_pages = page_table.shape[1]
    hist = state_conv.shape[2]
    ms = sb * ss

    w = dict(ffn1_norm=ffn1_norm, mix_norm=mix_norm, w_mix_in=w_mix_in, b_forget=b_forget, q_norm=q_norm,
             k_norm=k_norm, conv_w=conv_w, conv_b=conv_b, conv_ln_g=conv_ln_g, conv_ln_b=conv_ln_b,
             w_mix_out=w_mix_out, ffn2_norm=ffn2_norm, ple_norm=ple_norm, w_ple_gate=w_ple_gate,
             w_ple_proj=w_ple_proj,
             ffn1_w_in_bf=ffn1_w_in.astype(BF16), ffn1_w_out_bf=ffn1_w_out.astype(BF16),
             ffn2_w_in_bf=ffn2_w_in.astype(BF16), ffn2_w_out_bf=ffn2_w_out.astype(BF16))

    ckt = jnp.transpose(cache_k, (0, 1, 3, 4, 2)).reshape(depth * n_pool, ATT_W, page)
    cvt = jnp.transpose(cache_v, (0, 1, 3, 4, 2)).reshape(depth * n_pool, ATT_W, page)
    clft = jnp.transpose(cache_logf, (0, 1, 3, 2)).reshape(depth * n_pool, N_HEADS, page)
    state_t = jnp.transpose(state_conv, (0, 2, 1, 3))

    xp = x_prompt.reshape(b * t, d)
    xs = x_sample.reshape(ms, d)
    outs = {name: [] for name in ("kp", "vp", "fp", "cp", "ks", "vs", "fs", "cs")}
    heads_last = lambda a, lead: jnp.transpose(a.reshape(lead + (N_HEADS, HEAD_DIM, -1)),
                                               tuple(range(len(lead))) + (len(lead) + 2, len(lead), len(lead) + 1))

    for l in range(depth):
        lw = _layer_weights(l, w)

        xp = _ffn(xp, lw["ffn1_norm"], lw["ffn1_w_in"], lw["ffn1_w_out"], ROW_TILE)
        qb, ktb, vb, kt, vt, lft, ct, u = _mixin_call(xp.reshape(b, t, d), lw, ROW_TILE)
        attn = _attn_prompt(qb, ktb, vb, ct, lw["s_bound"], ATT_TILE)
        conv = _conv_prompt(u, lw["conv_w"], lw["conv_b"], lw["conv_ln_g"], lw["conv_ln_b"], ROW_TILE)
        xp = _finish_layer(xp, attn.reshape(b * t, ATT_W), conv.reshape(b * t, -1),
                           p_prompt[l].reshape(b * t, -1), lw, ROW_TILE)
        outs["kp"].append(heads_last(kt, (b,)))
        outs["vp"].append(heads_last(vt, (b,)))
        outs["fp"].append(jnp.transpose(lft, (0, 2, 1)))
        outs["cp"].append(u[:, t - hist:, :])

        xs = _ffn(xs, lw["ffn1_norm"], lw["ffn1_w_in"], lw["ffn1_w_out"], ms)
        qb, _, _, kt, vt, lft, _, u = _mixin_call(xs.reshape(1, ms, d), lw, ms)
        new_page = lambda a: jnp.pad(jnp.transpose(a[0].reshape(-1, sb, ss), (1, 0, 2)),
                                     ((0, 0), (0, 0), (0, page - ss)))
        pt = (page_table + l * n_pool).reshape(-1)
        attn = _sample_attn(pt, qb.reshape(sb, ss, ATT_W).astype(F32), new_page(kt), new_page(vt),
                            new_page(lft), ckt, cvt, clft, n_pages)
        u_t = jnp.transpose(u.reshape(sb, ss, -1), (1, 0, 2))
        conv = _conv_sample(state_t[l], u_t, lw["conv_w"], lw["conv_b"], lw["conv_ln_g"], lw["conv_ln_b"])
        conv = jnp.transpose(conv, (1, 0, 2)).reshape(ms, -1)
        xs = _finish_layer(xs, attn.reshape(ms, ATT_W).astype(BF16), conv.astype(BF16),
                           p_sample[l].reshape(ms, -1), lw, ms)
        outs["ks"].append(kt[0].T.reshape(sb, ss, N_HEADS, HEAD_DIM))
        outs["vs"].append(vt[0].T.reshape(sb, ss, N_HEADS, HEAD_DIM))
        outs["fs"].append(lft[0].T.reshape(sb, ss, N_HEADS))
        outs["cs"].append(jnp.transpose(jnp.concatenate([state_t[l], u_t], axis=0)[ss:], (1, 0, 2)))

    stack = lambda name: jnp.stack(outs[name])
    return (xp.reshape(b, t, d), xs.reshape(sb, ss, d), stack("kp"), stack("vp"), stack("fp"), stack("cp"),
            stack("ks"), stack("vs"), stack("fs"), stack("cs"))
```

```python
import functools

import jax
import jax.numpy as jnp
from jax import lax
from jax.experimental import pallas as pl
from jax.experimental.pallas import tpu as pltpu

F32 = jnp.float32
BF16 = jnp.bfloat16

EPS = 1e-6
N_HEADS = 8
HEAD_DIM = 64
ATT_W = N_HEADS * HEAD_DIM
CONV_W = 31
LANES = 128
SUBLANES = 8
BF16_ROWS = 16
MXU_WIDTH = 256
V7X_VMEM_BYTES = 64 * 2 ** 20
VMEM_LIMIT = V7X_VMEM_BYTES - 8 * 2 ** 20
NEG = -1e30
SKIP_THRESH = 106.0
FIXED_REF_LIMIT = 60.0

ROW_TILE = 512
ATT_TILE = 512
CONV_HALO = 32
CONV_CHUNK = 32
PAGES_PER_GROUP = 4
FFN_ROW_TILE = 1024


def _params(sem):
    return pltpu.CompilerParams(dimension_semantics=sem, vmem_limit_bytes=VMEM_LIMIT)


def _rms(x, g):
    return x * lax.rsqrt(jnp.mean(x * x, axis=-1, keepdims=True) + EPS) * g


def _sigmoid(x):
    return 1.0 / (1.0 + jnp.exp(-x))


def _silu(x):
    return x * _sigmoid(x)


def _log_sigmoid(x):
    return -(jnp.maximum(-x, 0.0) + jnp.log1p(jnp.exp(-jnp.abs(x))))


def _dot(a, b):
    return jnp.dot(a, b, preferred_element_type=F32)


def _dot_nt(a, b):
    return lax.dot_general(a, b, (((1,), (1,)), ((), ())), preferred_element_type=F32)


def _split3(x):
    hi = x.astype(BF16)
    r = x - hi.astype(F32)
    mid = r.astype(BF16)
    lo = (r - mid.astype(F32)).astype(BF16)
    return hi, mid, lo


def _lane_tile(x, width):
    return jnp.concatenate([x] * (width // LANES), axis=1)


def _swiglu_residual(x, g_ref, wg_ref, wu_ref, wo_ref):
    h = _rms(x, g_ref[...]).astype(BF16)
    acc = None
    for c in range(wo_ref.shape[0] // MXU_WIDTH):
        cols = slice(c * MXU_WIDTH, (c + 1) * MXU_WIDTH)
        gate = _dot(h, wg_ref[:, cols])
        up = _dot(h, wu_ref[:, cols])
        part = _dot((_silu(gate) * up).astype(BF16), wo_ref[cols, :])
        acc = part if acc is None else acc + part
    return x + 0.5 * acc


def _ffn_kernel(x_ref, g_ref, wg_ref, wu_ref, wo_ref, o_ref):
    o_ref[...] = _swiglu_residual(x_ref[...], g_ref, wg_ref, wu_ref, wo_ref)


def _resident(shape, index=None):
    index = index or (0,) * len(shape)
    return pl.BlockSpec(shape, lambda i: index, pipeline_mode=pl.Buffered(1))


def _ffn_specs(d, dff):
    return [_resident((1, d)), _resident((d, dff)), _resident((d, dff), (0, 1)), _resident((dff, d))]


def _ffn(x, g, w_in, w_out, tm):
    m, d = x.shape
    dff = w_out.shape[0]
    assert dff % MXU_WIDTH == 0
    return pl.pallas_call(
        _ffn_kernel,
        name="ffn",
        grid=(m // tm,),
        in_specs=[pl.BlockSpec((tm, d), lambda i: (i, 0))] + _ffn_specs(d, dff),
        out_specs=pl.BlockSpec((tm, d), lambda i: (i, 0)),
        out_shape=jax.ShapeDtypeStruct((m, d), F32),
        compiler_params=_params(("arbitrary",)),
    )(x, g.reshape(1, d), w_in, w_in, w_out)


def _lane_cumsum(x):
    n = x.shape[1]
    idx = lax.broadcasted_iota(jnp.int32, x.shape, 1)
    s = 1
    while s < n:
        x = x + jnp.where(idx >= s, pltpu.roll(x, s, axis=1), 0.0)
        s *= 2
    return x


def _mixin_kernel(x_ref, g_ref, wt_ref, wn_ref, bf_ref, qg_ref, kgt_ref, gmat_ref,
                  qb_ref, ktb_ref, vb_ref, kt_ref, vt_ref, lft_ref, ct_ref, u_ref, carry_ref):
    i = pl.program_id(1)
    tm = x_ref.shape[1]
    h = _rms(x_ref[0], g_ref[...]).astype(BF16)

    zt = _dot_nt(wt_ref[...], h)
    k3 = zt[:ATT_W].reshape(N_HEADS, HEAD_DIM, tm)
    ms = jnp.mean(k3 * k3, axis=1, keepdims=True)
    ktn = (k3 * lax.rsqrt(ms + EPS)).reshape(ATT_W, tm) * _lane_tile(kgt_ref[...], tm)
    kt_ref[0] = ktn
    ktb_ref[0] = ktn.astype(BF16)
    vt = zt[ATT_W:2 * ATT_W]
    vt_ref[0] = vt
    vb_ref[0] = vt.T.astype(BF16)

    lft = _log_sigmoid(zt[2 * ATT_W:2 * ATT_W + N_HEADS] + _lane_tile(bf_ref[...], tm))
    lft_ref[0] = lft

    @pl.when(i == 0)
    def _():
        carry_ref[...] = jnp.zeros_like(carry_ref)

    c = _lane_cumsum(lft) + carry_ref[:, :1]
    ct_ref[0] = c
    carry_ref[...] = jnp.broadcast_to(c[:, -1:], carry_ref.shape)

    z = _dot(h, wn_ref[...])
    q = z[:, :ATT_W]
    ms_q = _dot((q * q).astype(BF16), gmat_ref[...])
    qb_ref[0] = (q * lax.rsqrt(ms_q + EPS) * qg_ref[...] * (HEAD_DIM ** -0.5)).astype(BF16)
    cw = (z.shape[1] - ATT_W) // 2
    u_ref[0] = z[:, ATT_W:ATT_W + cw] * _sigmoid(z[:, ATT_W + cw:])


def _mixin(x, g, wt, wn, bfp, qg, kgt, gmat, tm):
    b, t, d = x.shape
    cw = (wn.shape[1] - ATT_W) // 2
    const = lambda shape: pl.BlockSpec(shape, lambda bi, i: (0,) * len(shape))
    row = lambda w: pl.BlockSpec((1, tm, w), lambda bi, i: (bi, i, 0))
    col = lambda r: pl.BlockSpec((1, r, tm), lambda bi, i: (bi, 0, i))
    return pl.pallas_call(
        _mixin_kernel,
        name="mixin",
        grid=(b, t // tm),
        in_specs=[row(d), const((1, d)), const(wt.shape), const(wn.shape), const(bfp.shape),
                  const(qg.shape), const(kgt.shape), const(gmat.shape)],
        out_specs=[row(ATT_W), col(ATT_W), row(ATT_W), col(ATT_W), col(ATT_W), col(N_HEADS),
                   col(N_HEADS), row(cw)],
        out_shape=[jax.ShapeDtypeStruct((b, t, ATT_W), BF16), jax.ShapeDtypeStruct((b, ATT_W, t), BF16),
                   jax.ShapeDtypeStruct((b, t, ATT_W), BF16), jax.ShapeDtypeStruct((b, ATT_W, t), F32),
                   jax.ShapeDtypeStruct((b, ATT_W, t), F32), jax.ShapeDtypeStruct((b, N_HEADS, t), F32),
                   jax.ShapeDtypeStruct((b, N_HEADS, t), F32), jax.ShapeDtypeStruct((b, t, cw), F32)],
        scratch_shapes=[pltpu.VMEM((N_HEADS, LANES), F32)],
        compiler_params=_params(("arbitrary", "arbitrary")),
    )(x, g.reshape(1, d), wt, wn, bfp, qg, kgt, gmat)


def _head_pair(q_ref):
    q = q_ref[0]
    first = lax.broadcasted_iota(jnp.int32, q.shape, 1) < HEAD_DIM
    zero = jnp.zeros_like(q)
    return (jnp.where(first, q, zero), jnp.where(first, zero, q)), first


def _attn_fixed_kernel(sb_ref, q_ref, kt_ref, v_ref, c2_ref, c8_ref, o_ref, acc_ref):
    hp = pl.program_id(1)
    i = pl.program_id(2)
    tq = q_ref.shape[1]
    t = kt_ref.shape[2]
    q_heads, first = _head_pair(q_ref)
    q0 = pl.multiple_of(i * tq, tq)
    c_base = c2_ref[0, 0, :, pl.ds(q0, LANES)][:, :1]

    slab = jnp.concatenate([c8_ref[0, :, pl.ds(q0, tq)], jnp.zeros((LANES - N_HEADS, tq), F32)], axis=0)
    by_row = slab.T
    head_lane = lax.broadcasted_iota(jnp.int32, by_row.shape, 1)
    shift = []
    for hh in range(2):
        mine = jnp.sum(jnp.where(head_lane == 2 * hp + hh, by_row, 0.0), axis=1, keepdims=True)
        shift.append(mine - c_base[hh:hh + 1, :] - sb_ref[0])

    c_all = c2_ref[0, 0]
    col = lax.broadcasted_iota(jnp.int32, c_all.shape, 1)
    block_end = jnp.logical_and(col % tq == tq - 1, col < q0)
    alive = (c_base - c_all) + 1e-5 * jnp.abs(c_all) >= -SKIP_THRESH
    oldest = jnp.min(jnp.where(jnp.logical_and(block_end, alive), col // tq, i))
    del t

    acc_ref[...] = jnp.zeros_like(acc_ref)
    one = jnp.ones((tq, LANES), BF16)

    def step(j, diagonal):
        k0 = pl.multiple_of(j * tq, tq)
        kt = kt_ref[0, :, pl.ds(k0, tq)]
        vb = v_ref[0, pl.ds(k0, tq), :]
        ck = c2_ref[0, 0, :, pl.ds(k0, tq)] - c_base
        first_k = lax.broadcasted_iota(jnp.int32, vb.shape, 1) < HEAD_DIM
        v_aug = (jnp.where(first_k, vb, one), jnp.where(first_k, one, vb))
        for hh in range(2):
            x = (_dot(q_heads[hh], kt) + shift[hh]) - ck[hh:hh + 1, :]
            if diagonal:
                r = lax.broadcasted_iota(jnp.int32, x.shape, 0)
                cidx = lax.broadcasted_iota(jnp.int32, x.shape, 1)
                x = jnp.where(cidx <= r, x, NEG)
            acc_ref[hh] += _dot(jnp.exp(x).astype(BF16), v_aug[hh])

    step(i, True)

    def body(n, carry):
        step(i - 1 - n, False)
        return carry

    lax.fori_loop(0, i - oldest, body, 0)

    a0, a1 = acc_ref[0], acc_ref[1]
    out = jnp.where(first, a0 / pltpu.roll(a0, HEAD_DIM, axis=1), a1 / pltpu.roll(a1, HEAD_DIM, axis=1))
    o_ref[0] = out.astype(o_ref.dtype)


def _attn_online_kernel(sb_ref, q_ref, kt_ref, v_ref, c2_ref, c8_ref, o_ref, m_ref, l_ref, acc_ref):
    del c8_ref
    i = pl.program_id(2)
    tq = q_ref.shape[1]
    q_heads, first = _head_pair(q_ref)
    q0 = pl.multiple_of(i * tq, tq)
    c_base = c2_ref[0, 0, :, pl.ds(q0, LANES)][:, :1]
    s_bound = sb_ref[0]

    m_ref[...] = jnp.full_like(m_ref, NEG)
    l_ref[...] = jnp.zeros_like(l_ref)
    acc_ref[...] = jnp.zeros_like(acc_ref)

    def step(j, diagonal):
        k0 = pl.multiple_of(j * tq, tq)
        kt = kt_ref[0, :, pl.ds(k0, tq)]
        vb = v_ref[0, pl.ds(k0, tq), :]
        ck = c2_ref[0, 0, :, pl.ds(k0, tq)] - c_base
        for hh in range(2):
            s = _dot(q_heads[hh], kt) - ck[hh:hh + 1, :]
            if diagonal:
                r = lax.broadcasted_iota(jnp.int32, s.shape, 0)
                cidx = lax.broadcasted_iota(jnp.int32, s.shape, 1)
                s = jnp.where(cidx <= r, s, NEG)
            m_prev = m_ref[hh]
            m_new = jnp.maximum(m_prev, jnp.max(s, axis=1, keepdims=True))
            p = jnp.exp(s - m_new)
            alpha = jnp.exp(m_prev - m_new)
            l_ref[hh] = alpha * l_ref[hh] + jnp.sum(p, axis=1, keepdims=True)
            acc_ref[hh] = alpha * acc_ref[hh] + _dot(p.astype(BF16), vb)
            m_ref[hh] = m_new

    def needed(j):
        k_last = pl.multiple_of(j * tq, tq) + tq - LANES
        c_last = c2_ref[0, 0, :, pl.ds(k_last, LANES)][:, LANES - 1:]
        best = s_bound - (c_last - c_base) + 1e-5 * jnp.abs(c_last)
        worst = NEG
        for hh in range(2):
            worst = jnp.maximum(worst, jnp.max(best[hh:hh + 1, :] - jnp.min(m_ref[hh])))
        return worst > -SKIP_THRESH

    step(i, True)

    def wanted(j):
        return jnp.logical_and(j >= 0, needed(jnp.maximum(j, 0))).astype(jnp.int32)

    def body(state):
        j, _ = state
        step(j, False)
        return j - 1, wanted(j - 1)

    lax.while_loop(lambda state: state[1] > 0, body, (i - 1, wanted(i - 1)))

    out = jnp.where(first, acc_ref[0] / l_ref[0], acc_ref[1] / l_ref[1])
    o_ref[0] = out.astype(o_ref.dtype)


def _attn_prompt(qb, ktb, vb, ct, s_bound, tq):
    b, t, _ = qb.shape
    pairs = N_HEADS // 2
    c4 = ct.reshape(b, pairs, 2, t)

    def call(body, name, scratch):
        grid_spec = pltpu.PrefetchScalarGridSpec(
            num_scalar_prefetch=1,
            grid=(b, pairs, t // tq),
            in_specs=[
                pl.BlockSpec((1, tq, LANES), lambda bi, hp, i, sb: (bi, i, hp)),
                pl.BlockSpec((1, LANES, t), lambda bi, hp, i, sb: (bi, hp, 0)),
                pl.BlockSpec((1, t, LANES), lambda bi, hp, i, sb: (bi, 0, hp)),
                pl.BlockSpec((1, 1, 2, t), lambda bi, hp, i, sb: (bi, hp, 0, 0)),
                pl.BlockSpec((1, N_HEADS, t), lambda bi, hp, i, sb: (bi, 0, 0)),
            ],
            out_specs=pl.BlockSpec((1, tq, LANES), lambda bi, hp, i, sb: (bi, i, hp)),
            scratch_shapes=scratch,
        )
        return pl.pallas_call(
            body,
            name=name,
            grid_spec=grid_spec,
            out_shape=jax.ShapeDtypeStruct((b, t, ATT_W), BF16),
            compiler_params=_params(("arbitrary", "arbitrary", "arbitrary")),
        )(s_bound, qb, ktb, vb, c4, ct)

    acc = pltpu.VMEM((2, tq, LANES), F32)
    stat = pltpu.VMEM((2, tq, 1), F32)
    return lax.cond(
        2.0 * s_bound[0] <= FIXED_REF_LIMIT,
        lambda: call(_attn_fixed_kernel, "attn_prompt", [acc]),
        lambda: call(_attn_online_kernel, "attn_prompt_online", [stat, stat, acc]),
    )


def _ln_silu(y, g, beta):
    mu = jnp.mean(y, axis=-1, keepdims=True)
    yc = y - mu
    var = jnp.mean(yc * yc, axis=-1, keepdims=True)
    return _silu(yc * lax.rsqrt(var + EPS) * g + beta)


def _conv_kernel(prev_ref, cur_ref, w_ref, b_ref, g_ref, beta_ref, o_ref, ext_ref):
    i = pl.program_id(1)
    tm = cur_ref.shape[1]
    halo = prev_ref[0]
    ext_ref[0, 0:CONV_HALO, :] = jnp.where(i == 0, jnp.zeros_like(halo), halo)
    ext_ref[0, CONV_HALO:, :] = cur_ref[0]
    span = tm + CONV_HALO - SUBLANES
    for s in range(1, SUBLANES):
        ext_ref[s, 0:span, :] = ext_ref[0, s:s + span, :]
    first = CONV_HALO - (CONV_W - 1)

    def chunk(ci, carry):
        r0 = pl.multiple_of(ci * CONV_CHUNK, CONV_CHUNK)
        acc = jnp.zeros((CONV_CHUNK, cur_ref.shape[2]), F32) + b_ref[...]
        for j in range(CONV_W):
            whole, s = divmod(first + j, SUBLANES)
            w_rows = jnp.concatenate([w_ref[j]] * (CONV_CHUNK // SUBLANES), axis=0)
            acc = acc + ext_ref[s, pl.ds(r0 + SUBLANES * whole, CONV_CHUNK), :] * w_rows
        o_ref[0, pl.ds(r0, CONV_CHUNK), :] = _ln_silu(acc, g_ref[...], beta_ref[...]).astype(o_ref.dtype)
        return carry

    lax.fori_loop(0, tm // CONV_CHUNK, chunk, 0)


def _conv_prompt(u, w, bias, g, beta, tm):
    b, t, c = u.shape
    per = tm // CONV_HALO
    const = lambda shape: pl.BlockSpec(shape, lambda bi, i: (0,) * len(shape))
    w_rows = jnp.broadcast_to(w[:, None, :], (CONV_W, SUBLANES, c))
    return pl.pallas_call(
        _conv_kernel,
        name="conv_prompt",
        grid=(b, t // tm),
        in_specs=[
            pl.BlockSpec((1, CONV_HALO, c), lambda bi, i: (bi, jnp.maximum(i * per - 1, 0), 0)),
            pl.BlockSpec((1, tm, c), lambda bi, i: (bi, i, 0)),
            const((CONV_W, SUBLANES, c)), const((1, c)), const((1, c)), const((1, c)),
        ],
        out_specs=pl.BlockSpec((1, tm, c), lambda bi, i: (bi, i, 0)),
        out_shape=jax.ShapeDtypeStruct((b, t, c), BF16),
        scratch_shapes=[pltpu.VMEM((SUBLANES, tm + CONV_HALO, c), F32)],
        compiler_params=_params(("arbitrary", "arbitrary")),
    )(u, u, w_rows, bias.reshape(1, c), g.reshape(1, c), beta.reshape(1, c))


def _conv_sample_kernel(state_ref, u_ref, w_ref, b_ref, g_ref, beta_ref, o_ref):
    hist = state_ref.shape[0]
    for t in range(u_ref.shape[0]):
        acc = jnp.zeros(u_ref.shape[1:], F32) + b_ref[...]
        for j in range(CONV_W):
            src = t + j
            row = state_ref[src] if src < hist else u_ref[src - hist]
            acc = acc + row * w_ref[j:j + 1, :]
        o_ref[t] = _ln_silu(acc, g_ref[...], beta_ref[...])


def _conv_sample(state, u, w, bias, g, beta):
    s, b, c = u.shape
    return pl.pallas_call(
        _conv_sample_kernel,
        name="conv_sample",
        out_shape=jax.ShapeDtypeStruct((s, b, c), F32),
        compiler_params=pltpu.CompilerParams(vmem_limit_bytes=VMEM_LIMIT),
    )(state, u, w, bias.reshape(1, c), g.reshape(1, c), beta.reshape(1, c))


def _post_kernel(x_ref, a_ref, c_ref, p_ref, wa_ref, wc_ref, g_ref, wg_ref, wu_ref, wo_ref,
                 gp_ref, wpg_ref, wpp_ref, o_ref):
    x = x_ref[...] + _dot(a_ref[...], wa_ref[...]) + _dot(c_ref[...], wc_ref[...])
    x = _swiglu_residual(x, g_ref, wg_ref, wu_ref, wo_ref)
    gate = _sigmoid(_dot(_rms(x, gp_ref[...]).astype(BF16), wpg_ref[...]))
    o_ref[...] = x + gate * _dot(p_ref[...].astype(BF16), wpp_ref[...])


def _post(x, attn, conv, p, lw, tm):
    m, d = x.shape
    dff = lw["ffn2_w_out"].shape[0]
    assert dff % MXU_WIDTH == 0
    row = lambda w: pl.BlockSpec((tm, w), lambda i: (i, 0))
    return pl.pallas_call(
        _post_kernel,
        name="post",
        grid=(m // tm,),
        in_specs=[row(d), row(attn.shape[1]), row(conv.shape[1]), row(p.shape[1]),
                  _resident(lw["wmo_a"].shape), _resident(lw["wmo_c"].shape)] + _ffn_specs(d, dff)
        + [_resident((1, d)), _resident(lw["wg"].shape), _resident(lw["wp"].shape)],
        out_specs=row(d),
        out_shape=jax.ShapeDtypeStruct((m, d), F32),
        compiler_params=_params(("arbitrary",)),
    )(x, attn, conv, p, lw["wmo_a"], lw["wmo_c"], lw["ffn2_norm"].reshape(1, d), lw["ffn2_w_in"],
      lw["ffn2_w_in"], lw["ffn2_w_out"], lw["ple_norm"].reshape(1, d), lw["wg"], lw["wp"])


def _sample_attn_kernel(n_pages, pt_ref, sb_ref, q_ref, kn_ref, vn_ref, lfn_ref, ck_hbm, cv_hbm, clf_hbm,
                        o_ref, kbuf, vbuf, lfbuf, sem, qbd_ref, m_ref, l_ref, acc_ref, r_ref):
    bi = pl.program_id(0)
    npg = PAGES_PER_GROUP
    n_groups = n_pages // npg
    s_q = q_ref.shape[1]
    rows = s_q * N_HEADS
    page = kn_ref.shape[2]

    def group_copies(g, slot):
        copies = []
        for pi in range(npg):
            row = pt_ref[bi * n_pages + (n_pages - 1 - (g * npg + pi))]
            copies.append(pltpu.make_async_copy(ck_hbm.at[row], kbuf.at[slot, pi], sem.at[slot, 0]))
            copies.append(pltpu.make_async_copy(cv_hbm.at[row], vbuf.at[slot, pi], sem.at[slot, 1]))
            copies.append(pltpu.make_async_copy(clf_hbm.at[row], lfbuf.at[slot, pi], sem.at[slot, 2]))
        return copies

    for copy in group_copies(0, 0):
        copy.start()

    row_i = lax.broadcasted_iota(jnp.int32, (rows, ATT_W), 0)
    lane_i = lax.broadcasted_iota(jnp.int32, (rows, ATT_W), 1)
    own_head = (lane_i // HEAD_DIM) == (row_i % N_HEADS)
    q = q_ref[0]
    q_rep = jnp.broadcast_to(q[:, None, :], (s_q, N_HEADS, ATT_W)).reshape(rows, ATT_W)
    qbd_ref[...] = jnp.where(own_head, q_rep, 0.0).astype(BF16)
    m_ref[...] = jnp.full_like(m_ref, NEG)
    l_ref[...] = jnp.zeros_like(l_ref)
    acc_ref[...] = jnp.zeros_like(acc_ref)

    jj = lax.broadcasted_iota(jnp.int32, (page, 2 * page), 0)
    ss = lax.broadcasted_iota(jnp.int32, (page, 2 * page), 1)
    later_and_all = jnp.where(jnp.logical_or(jj > ss, ss >= page), 1.0, 0.0).astype(BF16)

    def forget_sums(lft_page):
        both = sum(_dot(part, later_and_all) for part in _split3(lft_page))
        return both[:, :page], both[:, page:]

    def logits(kt_page, bias):
        return _dot(qbd_ref[...], kt_page.astype(BF16)) + jnp.concatenate([bias] * s_q, axis=0)

    def attend(s, vt_pages):
        m_prev = m_ref[...]
        m_new = jnp.maximum(m_prev, jnp.max(s, axis=1, keepdims=True))
        p = jnp.exp(s - m_new).astype(BF16)
        alpha = jnp.exp(m_prev - m_new)
        l_ref[...] = alpha * l_ref[...] + jnp.sum(p.astype(F32), axis=1, keepdims=True)
        pv = sum(_dot_nt(p[:, n * page:(n + 1) * page], vt.astype(BF16)) for n, vt in enumerate(vt_pages))
        acc_ref[...] = alpha * acc_ref[...] + pv
        m_ref[...] = m_new

    suffix, total = forget_sums(lfn_ref[0])
    s_new = logits(kn_ref[0], suffix)
    key_i = lax.broadcasted_iota(jnp.int32, s_new.shape, 1)
    q_i = lax.broadcasted_iota(jnp.int32, s_new.shape, 0) // N_HEADS
    attend(jnp.where(key_i <= q_i, s_new, NEG), [vn_ref[0]])
    r_ref[...] = total

    def body(state):
        g, _ = state
        slot = g % 2
        for copy in group_copies(g, slot):
            copy.wait()
        run = r_ref[...]
        biases = []
        for pi in range(npg):
            suffix, total = forget_sums(lfbuf[slot, pi])
            biases.append(suffix + run)
            run = run + total
        r_ref[...] = run
        reach = sb_ref[0] + jnp.concatenate([run + 1e-5 * jnp.abs(run)] * s_q, axis=0) - m_ref[...]
        more = jnp.logical_and(g + 1 < n_groups, jnp.max(reach) >= -SKIP_THRESH)

        @pl.when(more)
        def _():
            for copy in group_copies(g + 1, 1 - slot):
                copy.start()

        s = jnp.concatenate([logits(kbuf[slot, pi], biases[pi]) for pi in range(npg)], axis=1)
        attend(s, [vbuf[slot, pi] for pi in range(npg)])
        return g + 1, more.astype(jnp.int32)

    lax.while_loop(lambda state: state[1] > 0, body, (jnp.int32(0), jnp.int32(1)))

    out = jnp.where(own_head, acc_ref[...] / l_ref[...], 0.0)
    o_ref[0] = jnp.sum(out.reshape(s_q, N_HEADS, ATT_W), axis=1)


def _sample_attn(pt, s_bound, q, kt_new, vt_new, lft_new, cache_kt, cache_vt, cache_lft, n_pages):
    b, s_q, _ = q.shape
    page = cache_kt.shape[2]
    npg = PAGES_PER_GROUP
    assert n_pages % npg == 0

    def per_batch(shape):
        return pl.BlockSpec((1,) + shape, lambda bi, pt_ref, sb_ref: (bi, 0, 0))

    in_hbm = pl.BlockSpec(memory_space=pl.ANY)
    rows = s_q * N_HEADS
    grid_spec = pltpu.PrefetchScalarGridSpec(
        num_scalar_prefetch=2,
        grid=(b,),
        in_specs=[per_batch((s_q, ATT_W)), per_batch((ATT_W, page)), per_batch((ATT_W, page)),
                  per_batch((N_HEADS, page)), in_hbm, in_hbm, in_hbm],
        out_specs=per_batch((s_q, ATT_W)),
        scratch_shapes=[pltpu.VMEM((2, npg, ATT_W, page), F32), pltpu.VMEM((2, npg, ATT_W, page), F32),
                        pltpu.VMEM((2, npg, N_HEADS, page), F32), pltpu.SemaphoreType.DMA((2, 3)),
                        pltpu.VMEM((rows, ATT_W), BF16), pltpu.VMEM((rows, 1), F32),
                        pltpu.VMEM((rows, 1), F32), pltpu.VMEM((rows, ATT_W), F32),
                        pltpu.VMEM((N_HEADS, page), F32)],
    )
    return pl.pallas_call(
        functools.partial(_sample_attn_kernel, n_pages),
        name="attn_sample",
        grid_spec=grid_spec,
        out_shape=jax.ShapeDtypeStruct((b, s_q, ATT_W), F32),
        compiler_params=_params(("arbitrary",)),
    )(pt, s_bound, q, kt_new, vt_new, lft_new, cache_kt, cache_vt, cache_lft)


def _layer_weights(l, w):
    wmi = w["w_mix_in"][l]
    d = wmi.shape[0]
    n_qkv = 3 * ATT_W
    pad_rows = BF16_ROWS - N_HEADS
    wt = jnp.concatenate([wmi[:, ATT_W:n_qkv + N_HEADS].T, jnp.zeros((pad_rows, d), F32)], axis=0)
    wn = jnp.concatenate([wmi[:, :ATT_W], wmi[:, n_qkv + N_HEADS:]], axis=1)
    bfp = jnp.broadcast_to(w["b_forget"][l][:, None], (N_HEADS, LANES))
    kgt = jnp.broadcast_to(jnp.tile(w["k_norm"][l], N_HEADS)[:, None], (ATT_W, LANES))
    head_of = jnp.arange(ATT_W) // HEAD_DIM
    gmat = jnp.where(head_of[:, None] == head_of[None, :], 1.0 / HEAD_DIM, 0.0).astype(BF16)
    s_bound = (HEAD_DIM ** 0.5) * jnp.max(jnp.abs(w["q_norm"][l])) * jnp.max(jnp.abs(w["k_norm"][l])) * 1.02
    wmo = w["w_mix_out"][l].astype(BF16)
    return dict(
        ffn1_norm=w["ffn1_norm"][l], ffn1_w_in=w["ffn1_w_in_bf"][l], ffn1_w_out=w["ffn1_w_out_bf"][l],
        mix_norm=w["mix_norm"][l], wt=wt.astype(BF16), wn=wn.astype(BF16), bfp=bfp,
        qg=jnp.tile(w["q_norm"][l], N_HEADS).reshape(1, ATT_W), kgt=kgt, gmat=gmat,
        s_bound=s_bound.reshape(1).astype(F32),
        conv_w=w["conv_w"][l], conv_b=w["conv_b"][l], conv_ln_g=w["conv_ln_g"][l],
        conv_ln_b=w["conv_ln_b"][l], wmo_a=wmo[:ATT_W], wmo_c=wmo[ATT_W:],
        ffn2_norm=w["ffn2_norm"][l], ffn2_w_in=w["ffn2_w_in_bf"][l], ffn2_w_out=w["ffn2_w_out_bf"][l],
        ple_norm=w["ple_norm"][l], wg=w["w_ple_gate"][l].astype(BF16), wp=w["w_ple_proj"][l].astype(BF16),
    )


def _mixin_call(x3, lw, tm):
    return _mixin(x3, lw["mix_norm"], lw["wt"], lw["wn"], lw["bfp"], lw["qg"], lw["kgt"], lw["gmat"], tm)


def kernel(x_prompt, x_sample, p_prompt, p_sample, cache_k, cache_v, cache_logf, state_conv, page_table, ffn1_norm, ffn1_w_in, ffn1_w_out, mix_norm, w_mix_in, b_forget, q_norm, k_norm, conv_w, conv_b, conv_ln_g, conv_ln_b, w_mix_out, ffn2_norm, ffn2_w_in, ffn2_w_out, ple_norm, w_ple_gate, w_ple_proj):
    depth = ffn1_norm.shape[0]
    b, t, d = x_prompt.shape
    sb, ss, _ = x_sample.shape
    n_pool, page = cache_k.shape[1], cache_k.shape[2]
    n_pages = page_table.shape[1]
    hist = state_conv.shape[2]
    ms = sb * ss

    w = dict(ffn1_norm=ffn1_norm, mix_norm=mix_norm, w_mix_in=w_mix_in, b_forget=b_forget, q_norm=q_norm,
             k_norm=k_norm, conv_w=conv_w, conv_b=conv_b, conv_ln_g=conv_ln_g, conv_ln_b=conv_ln_b,
             w_mix_out=w_mix_out, ffn2_norm=ffn2_norm, ple_norm=ple_norm, w_ple_gate=w_ple_gate,
             w_ple_proj=w_ple_proj,
             ffn1_w_in_bf=ffn1_w_in.astype(BF16), ffn1_w_out_bf=ffn1_w_out.astype(BF16),
             ffn2_w_in_bf=ffn2_w_in.astype(BF16), ffn2_w_out_bf=ffn2_w_out.astype(BF16))

    ckt = jnp.transpose(cache_k, (0, 1, 3, 4, 2)).reshape(depth * n_pool, ATT_W, page)
    cvt = jnp.transpose(cache_v, (0, 1, 3, 4, 2)).reshape(depth * n_pool, ATT_W, page)
    clft = jnp.transpose(cache_logf, (0, 1, 3, 2)).reshape(depth * n_pool, N_HEADS, page)
    state_t = jnp.transpose(state_conv, (0, 2, 1, 3))

    xp = x_prompt.reshape(b * t, d)
    xs = x_sample.reshape(ms, d)
    outs = {name: [] for name in ("kp", "vp", "fp", "cp", "ks", "vs", "fs", "cs")}
    heads_last = lambda a, lead: jnp.transpose(a.reshape(lead + (N_HEADS, HEAD_DIM, -1)),
                                               tuple(range(len(lead))) + (len(lead) + 2, len(lead), len(lead) + 1))

    for l in range(depth):
        lw = _layer_weights(l, w)

        xp = _ffn(xp, lw["ffn1_norm"], lw["ffn1_w_in"], lw["ffn1_w_out"], FFN_ROW_TILE)
        qb, ktb, vb, kt, vt, lft, ct, u = _mixin_call(xp.reshape(b, t, d), lw, ROW_TILE)
        attn = _attn_prompt(qb, ktb, vb, ct, lw["s_bound"], ATT_TILE)
        conv = _conv_prompt(u, lw["conv_w"], lw["conv_b"], lw["conv_ln_g"], lw["conv_ln_b"], ROW_TILE)
        xp = _post(xp, attn.reshape(b * t, ATT_W), conv.reshape(b * t, -1),
                   p_prompt[l].reshape(b * t, -1), lw, ROW_TILE)
        outs["kp"].append(heads_last(kt, (b,)))
        outs["vp"].append(heads_last(vt, (b,)))
        outs["fp"].append(jnp.transpose(lft, (0, 2, 1)))
        outs["cp"].append(u[:, t - hist:, :])

        xs = _ffn(xs, lw["ffn1_norm"], lw["ffn1_w_in"], lw["ffn1_w_out"], ms)
        qb, _, _, kt, vt, lft, _, u = _mixin_call(xs.reshape(1, ms, d), lw, ms)
        new_page = lambda a: jnp.pad(jnp.transpose(a[0].reshape(-1, sb, ss), (1, 0, 2)),
                                     ((0, 0), (0, 0), (0, page - ss)))
        pt = (page_table + l * n_pool).reshape(-1)
        attn = _sample_attn(pt, lw["s_bound"], qb.reshape(sb, ss, ATT_W).astype(F32), new_page(kt),
                            new_page(vt), new_page(lft), ckt, cvt, clft, n_pages)
        u_t = jnp.transpose(u.reshape(sb, ss, -1), (1, 0, 2))
        conv = _conv_sample(state_t[l], u_t, lw["conv_w"], lw["conv_b"], lw["conv_ln_g"], lw["conv_ln_b"])
        conv = jnp.transpose(conv, (1, 0, 2)).reshape(ms, -1)
        xs = _post(xs, attn.reshape(ms, ATT_W).astype(BF16), conv.astype(BF16),
                   p_sample[l].reshape(ms, -1), lw, ms)
        outs["ks"].append(kt[0].T.reshape(sb, ss, N_HEADS, HEAD_DIM))
        outs["vs"].append(vt[0].T.reshape(sb, ss, N_HEADS, HEAD_DIM))
        outs["fs"].append(lft[0].T.reshape(sb, ss, N_HEADS))
        outs["cs"].append(jnp.transpose(jnp.concatenate([state_t[l], u_t], axis=0)[ss:], (1, 0, 2)))

    stack = lambda name: jnp.stack(outs[name])
    return (xp.reshape(b, t, d), xs.reshape(sb, ss, d), stack("kp"), stack("vp"), stack("fp"), stack("cp"),
            stack("ks"), stack("vs"), stack("fs"), stack("cs"))
```

```python
import functools

import jax
import jax.numpy as jnp
from jax import lax
from jax.experimental import pallas as pl
from jax.experimental.pallas import tpu as pltpu

F32 = jnp.float32
BF16 = jnp.bfloat16

EPS = 1e-6
N_HEADS = 8
HEAD_DIM = 64
ATT_W = N_HEADS * HEAD_DIM
CONV_W = 31
LANES = 128
SUBLANES = 8
BF16_ROWS = 16
MXU_WIDTH = 256
V7X_VMEM_BYTES = 64 * 2 ** 20
VMEM_LIMIT = V7X_VMEM_BYTES - 8 * 2 ** 20
NEG = -1e30
SKIP_THRESH = 106.0
SKIP_THRESH_BF16 = 94.0
FIXED_REF_LIMIT = 60.0

ROW_TILE = 512
ATT_TILE = 512
CONV_HALO = 32
CONV_CHUNK = 32
PAGES_PER_GROUP = 8
FFN_ROW_TILE = 1024


def _params(sem):
    return pltpu.CompilerParams(dimension_semantics=sem, vmem_limit_bytes=VMEM_LIMIT)


def _rms(x, g):
    return x * lax.rsqrt(jnp.mean(x * x, axis=-1, keepdims=True) + EPS) * g


def _sigmoid(x):
    return 1.0 / (1.0 + jnp.exp(-x))


def _silu(x):
    return x * _sigmoid(x)


def _log_sigmoid(x):
    return -(jnp.maximum(-x, 0.0) + jnp.log1p(jnp.exp(-jnp.abs(x))))


def _dot(a, b):
    return jnp.dot(a, b, preferred_element_type=F32)


def _dot_nt(a, b):
    return lax.dot_general(a, b, (((1,), (1,)), ((), ())), preferred_element_type=F32)


def _split3(x):
    hi = x.astype(BF16)
    r = x - hi.astype(F32)
    mid = r.astype(BF16)
    lo = (r - mid.astype(F32)).astype(BF16)
    return hi, mid, lo


def _lane_tile(x, width):
    return jnp.concatenate([x] * (width // LANES), axis=1)


def _swiglu_residual(x, g_ref, wg_ref, wu_ref, wo_ref):
    h = _rms(x, g_ref[...]).astype(BF16)
    acc = None
    for c in range(wo_ref.shape[0] // MXU_WIDTH):
        cols = slice(c * MXU_WIDTH, (c + 1) * MXU_WIDTH)
        gate = _dot(h, wg_ref[:, cols])
        up = _dot(h, wu_ref[:, cols])
        part = _dot((_silu(gate) * up).astype(BF16), wo_ref[cols, :])
        acc = part if acc is None else acc + part
    return x + 0.5 * acc


def _ffn_kernel(x_ref, g_ref, wg_ref, wu_ref, wo_ref, o_ref):
    o_ref[...] = _swiglu_residual(x_ref[...], g_ref, wg_ref, wu_ref, wo_ref)


def _resident(shape, index=None):
    index = index or (0,) * len(shape)
    return pl.BlockSpec(shape, lambda i: index, pipeline_mode=pl.Buffered(1))


def _ffn_specs(d, dff, layer):
    return [_resident((1, d)), _resident((None, d, dff), (layer, 0, 0)),
            _resident((None, d, dff), (layer, 0, 1)), _resident((None, dff, d), (layer, 0, 0))]


def _ffn(x, g, w_in, w_out, layer, tm):
    m, d = x.shape
    dff = w_out.shape[1]
    assert dff % MXU_WIDTH == 0
    return pl.pallas_call(
        _ffn_kernel,
        name="ffn",
        grid=(m // tm,),
        in_specs=[pl.BlockSpec((tm, d), lambda i: (i, 0))] + _ffn_specs(d, dff, layer),
        out_specs=pl.BlockSpec((tm, d), lambda i: (i, 0)),
        out_shape=jax.ShapeDtypeStruct((m, d), F32),
        compiler_params=_params(("arbitrary",)),
    )(x, g.reshape(1, d), w_in, w_in, w_out)


def _lane_cumsum(x):
    n = x.shape[1]
    idx = lax.broadcasted_iota(jnp.int32, x.shape, 1)
    s = 1
    while s < n:
        x = x + jnp.where(idx >= s, pltpu.roll(x, s, axis=1), 0.0)
        s *= 2
    return x


def _mixin_kernel(x_ref, g_ref, wt_ref, wn_ref, bf_ref, qg_ref, kgt_ref, gmat_ref,
                  qb_ref, ktb_ref, vb_ref, kt_ref, vt_ref, lft_ref, ct_ref, u_ref, carry_ref):
    i = pl.program_id(1)
    tm = x_ref.shape[1]
    h = _rms(x_ref[0], g_ref[...]).astype(BF16)

    zt = _dot_nt(wt_ref[...], h)
    k3 = zt[:ATT_W].reshape(N_HEADS, HEAD_DIM, tm)
    ms = jnp.mean(k3 * k3, axis=1, keepdims=True)
    ktn = (k3 * lax.rsqrt(ms + EPS)).reshape(ATT_W, tm) * _lane_tile(kgt_ref[...], tm)
    kt_ref[0] = ktn
    ktb_ref[0] = ktn.astype(BF16)
    vt = zt[ATT_W:2 * ATT_W]
    vt_ref[0] = vt
    vb_ref[0] = vt.T.astype(BF16)

    lft = _log_sigmoid(zt[2 * ATT_W:2 * ATT_W + N_HEADS] + _lane_tile(bf_ref[...], tm))
    lft_ref[0] = lft

    @pl.when(i == 0)
    def _():
        carry_ref[...] = jnp.zeros_like(carry_ref)

    c = _lane_cumsum(lft) + carry_ref[:, :1]
    ct_ref[0] = c
    carry_ref[...] = jnp.broadcast_to(c[:, -1:], carry_ref.shape)

    z = _dot(h, wn_ref[...])
    q = z[:, :ATT_W]
    ms_q = _dot((q * q).astype(BF16), gmat_ref[...])
    qb_ref[0] = (q * lax.rsqrt(ms_q + EPS) * qg_ref[...] * (HEAD_DIM ** -0.5)).astype(BF16)
    cw = (z.shape[1] - ATT_W) // 2
    u_ref[0] = z[:, ATT_W:ATT_W + cw] * _sigmoid(z[:, ATT_W + cw:])


def _mixin(x, g, wt, wn, bfp, qg, kgt, gmat, tm):
    b, t, d = x.shape
    cw = (wn.shape[1] - ATT_W) // 2
    const = lambda shape: pl.BlockSpec(shape, lambda bi, i: (0,) * len(shape))
    row = lambda w: pl.BlockSpec((1, tm, w), lambda bi, i: (bi, i, 0))
    col = lambda r: pl.BlockSpec((1, r, tm), lambda bi, i: (bi, 0, i))
    return pl.pallas_call(
        _mixin_kernel,
        name="mixin",
        grid=(b, t // tm),
        in_specs=[row(d), const((1, d)), const(wt.shape), const(wn.shape), const(bfp.shape),
                  const(qg.shape), const(kgt.shape), const(gmat.shape)],
        out_specs=[row(ATT_W), col(ATT_W), row(ATT_W), col(ATT_W), col(ATT_W), col(N_HEADS),
                   col(N_HEADS), row(cw)],
        out_shape=[jax.ShapeDtypeStruct((b, t, ATT_W), BF16), jax.ShapeDtypeStruct((b, ATT_W, t), BF16),
                   jax.ShapeDtypeStruct((b, t, ATT_W), BF16), jax.ShapeDtypeStruct((b, ATT_W, t), F32),
                   jax.ShapeDtypeStruct((b, ATT_W, t), F32), jax.ShapeDtypeStruct((b, N_HEADS, t), F32),
                   jax.ShapeDtypeStruct((b, N_HEADS, t), F32), jax.ShapeDtypeStruct((b, t, cw), F32)],
        scratch_shapes=[pltpu.VMEM((N_HEADS, LANES), F32)],
        compiler_params=_params(("arbitrary", "arbitrary")),
    )(x, g.reshape(1, d), wt, wn, bfp, qg, kgt, gmat)


def _head_pair(q_ref):
    q = q_ref[0]
    first = lax.broadcasted_iota(jnp.int32, q.shape, 1) < HEAD_DIM
    zero = jnp.zeros_like(q)
    return (jnp.where(first, q, zero), jnp.where(first, zero, q)), first


def _attn_fixed_kernel(sb_ref, q_ref, kt_ref, v_ref, c2_ref, c8_ref, o_ref, acc_ref):
    hp = pl.program_id(1)
    i = pl.program_id(2)
    tq = q_ref.shape[1]
    q_heads, first = _head_pair(q_ref)
    q0 = pl.multiple_of(i * tq, tq)
    c_base = c2_ref[0, 0, :, pl.ds(q0, LANES)][:, :1]

    slab = jnp.concatenate([c8_ref[0, :, pl.ds(q0, tq)], jnp.zeros((LANES - N_HEADS, tq), F32)], axis=0)
    by_row = slab.T
    head_lane = lax.broadcasted_iota(jnp.int32, by_row.shape, 1)
    shift = []
    for hh in range(2):
        mine = jnp.sum(jnp.where(head_lane == 2 * hp + hh, by_row, 0.0), axis=1, keepdims=True)
        shift.append(mine - c_base[hh:hh + 1, :] - sb_ref[0])

    c_all = c2_ref[0, 0]
    col = lax.broadcasted_iota(jnp.int32, c_all.shape, 1)
    block_end = jnp.logical_and(col % tq == tq - 1, col < q0)
    alive = (c_base - c_all) + 1e-5 * jnp.abs(c_all) >= -SKIP_THRESH_BF16
    oldest = jnp.where(jnp.logical_and(block_end, alive), col // tq, i)
    reach = [i - jnp.min(oldest[hh:hh + 1, :]) for hh in range(2)]
    shared = jnp.minimum(reach[0], reach[1])

    acc_ref[...] = jnp.zeros_like(acc_ref)
    one = jnp.ones((tq, LANES), BF16)

    def step(j, diagonal, heads):
        k0 = pl.multiple_of(j * tq, tq)
        kt = kt_ref[0, :, pl.ds(k0, tq)]
        vb = v_ref[0, pl.ds(k0, tq), :]
        ck = c2_ref[0, 0, :, pl.ds(k0, tq)] - c_base
        first_k = lax.broadcasted_iota(jnp.int32, vb.shape, 1) < HEAD_DIM
        for hh in heads:
            x = (_dot(q_heads[hh], kt) + shift[hh]) - ck[hh:hh + 1, :]
            if diagonal:
                r = lax.broadcasted_iota(jnp.int32, x.shape, 0)
                cidx = lax.broadcasted_iota(jnp.int32, x.shape, 1)
                x = jnp.where(cidx <= r, x, NEG)
            v_aug = jnp.where(first_k, vb, one) if hh == 0 else jnp.where(first_k, one, vb)
            acc_ref[hh] += _dot(jnp.exp(x).astype(BF16), v_aug)

    step(i, True, (0, 1))

    def sweep(start, stop, heads):
        def body(n, carry):
            step(i - 1 - n, False, heads)
            return carry
        lax.fori_loop(start, stop, body, 0)

    sweep(0, shared, (0, 1))
    sweep(shared, reach[0], (0,))
    sweep(shared, reach[1], (1,))

    a0, a1 = acc_ref[0], acc_ref[1]
    out = jnp.where(first, a0 / pltpu.roll(a0, HEAD_DIM, axis=1), a1 / pltpu.roll(a1, HEAD_DIM, axis=1))
    o_ref[0] = out.astype(o_ref.dtype)


def _attn_online_kernel(sb_ref, q_ref, kt_ref, v_ref, c2_ref, c8_ref, o_ref, m_ref, l_ref, acc_ref):
    del c8_ref
    i = pl.program_id(2)
    tq = q_ref.shape[1]
    q_heads, first = _head_pair(q_ref)
    q0 = pl.multiple_of(i * tq, tq)
    c_base = c2_ref[0, 0, :, pl.ds(q0, LANES)][:, :1]
    s_bound = sb_ref[0]

    m_ref[...] = jnp.full_like(m_ref, NEG)
    l_ref[...] = jnp.zeros_like(l_ref)
    acc_ref[...] = jnp.zeros_like(acc_ref)

    def step(j, diagonal):
        k0 = pl.multiple_of(j * tq, tq)
        kt = kt_ref[0, :, pl.ds(k0, tq)]
        vb = v_ref[0, pl.ds(k0, tq), :]
        ck = c2_ref[0, 0, :, pl.ds(k0, tq)] - c_base
        for hh in range(2):
            s = _dot(q_heads[hh], kt) - ck[hh:hh + 1, :]
            if diagonal:
                r = lax.broadcasted_iota(jnp.int32, s.shape, 0)
                cidx = lax.broadcasted_iota(jnp.int32, s.shape, 1)
                s = jnp.where(cidx <= r, s, NEG)
            m_prev = m_ref[hh]
            m_new = jnp.maximum(m_prev, jnp.max(s, axis=1, keepdims=True))
            p = jnp.exp(s - m_new)
            alpha = jnp.exp(m_prev - m_new)
            l_ref[hh] = alpha * l_ref[hh] + jnp.sum(p, axis=1, keepdims=True)
            acc_ref[hh] = alpha * acc_ref[hh] + _dot(p.astype(BF16), vb)
            m_ref[hh] = m_new

    def needed(j):
        k_last = pl.multiple_of(j * tq, tq) + tq - LANES
        c_last = c2_ref[0, 0, :, pl.ds(k_last, LANES)][:, LANES - 1:]
        best = s_bound - (c_last - c_base) + 1e-5 * jnp.abs(c_last)
        worst = NEG
        for hh in range(2):
            worst = jnp.maximum(worst, jnp.max(best[hh:hh + 1, :] - jnp.min(m_ref[hh])))
        return worst > -SKIP_THRESH

    step(i, True)

    def wanted(j):
        return jnp.logical_and(j >= 0, needed(jnp.maximum(j, 0))).astype(jnp.int32)

    def body(state):
        j, _ = state
        step(j, False)
        return j - 1, wanted(j - 1)

    lax.while_loop(lambda state: state[1] > 0, body, (i - 1, wanted(i - 1)))

    out = jnp.where(first, acc_ref[0] / l_ref[0], acc_ref[1] / l_ref[1])
    o_ref[0] = out.astype(o_ref.dtype)


def _attn_prompt(qb, ktb, vb, ct, s_bound, tq):
    b, t, _ = qb.shape
    pairs = N_HEADS // 2
    c4 = ct.reshape(b, pairs, 2, t)

    def call(body, name, scratch):
        grid_spec = pltpu.PrefetchScalarGridSpec(
            num_scalar_prefetch=1,
            grid=(b, pairs, t // tq),
            in_specs=[
                pl.BlockSpec((1, tq, LANES), lambda bi, hp, i, sb: (bi, i, hp)),
                pl.BlockSpec((1, LANES, t), lambda bi, hp, i, sb: (bi, hp, 0)),
                pl.BlockSpec((1, t, LANES), lambda bi, hp, i, sb: (bi, 0, hp)),
                pl.BlockSpec((1, 1, 2, t), lambda bi, hp, i, sb: (bi, hp, 0, 0)),
                pl.BlockSpec((1, N_HEADS, t), lambda bi, hp, i, sb: (bi, 0, 0)),
            ],
            out_specs=pl.BlockSpec((1, tq, LANES), lambda bi, hp, i, sb: (bi, i, hp)),
            scratch_shapes=scratch,
        )
        return pl.pallas_call(
            body,
            name=name,
            grid_spec=grid_spec,
            out_shape=jax.ShapeDtypeStruct((b, t, ATT_W), BF16),
            compiler_params=_params(("arbitrary", "arbitrary", "arbitrary")),
        )(s_bound, qb, ktb, vb, c4, ct)

    acc = pltpu.VMEM((2, tq, LANES), F32)
    stat = pltpu.VMEM((2, tq, 1), F32)
    return lax.cond(
        2.0 * s_bound[0] <= FIXED_REF_LIMIT,
        lambda: call(_attn_fixed_kernel, "attn_prompt", [acc]),
        lambda: call(_attn_online_kernel, "attn_prompt_online", [stat, stat, acc]),
    )


def _ln_silu(y, g, beta):
    mu = jnp.mean(y, axis=-1, keepdims=True)
    yc = y - mu
    var = jnp.mean(yc * yc, axis=-1, keepdims=True)
    return _silu(yc * lax.rsqrt(var + EPS) * g + beta)


def _conv_kernel(prev_ref, cur_ref, w_ref, b_ref, g_ref, beta_ref, o_ref, ext_ref, y_ref):
    i = pl.program_id(1)
    tm = cur_ref.shape[1]
    halo = prev_ref[0]
    ext_ref[0, 0:CONV_HALO, :] = jnp.where(i == 0, jnp.zeros_like(halo), halo)
    ext_ref[0, CONV_HALO:, :] = cur_ref[0]
    span = tm + CONV_HALO - SUBLANES
    for s in range(1, SUBLANES):
        ext_ref[s, 0:span, :] = ext_ref[0, s:s + span, :]
    first = CONV_HALO - (CONV_W - 1)

    def chunk(ci, carry):
        r0 = pl.multiple_of(ci * CONV_CHUNK, CONV_CHUNK)
        acc = jnp.zeros((CONV_CHUNK, cur_ref.shape[2]), F32) + b_ref[...]
        for j in range(CONV_W):
            whole, s = divmod(first + j, SUBLANES)
            w_rows = jnp.concatenate([w_ref[j]] * (CONV_CHUNK // SUBLANES), axis=0)
            acc = acc + ext_ref[s, pl.ds(r0 + SUBLANES * whole, CONV_CHUNK), :] * w_rows
        y_ref[pl.ds(r0, CONV_CHUNK), :] = acc
        return carry

    lax.fori_loop(0, tm // CONV_CHUNK, chunk, 0)
    o_ref[0] = _ln_silu(y_ref[...], g_ref[...], beta_ref[...]).astype(o_ref.dtype)


def _conv_prompt(u, w, bias, g, beta, tm):
    b, t, c = u.shape
    per = tm // CONV_HALO
    const = lambda shape: pl.BlockSpec(shape, lambda bi, i: (0,) * len(shape))
    w_rows = jnp.broadcast_to(w[:, None, :], (CONV_W, SUBLANES, c))
    return pl.pallas_call(
        _conv_kernel,
        name="conv_prompt",
        grid=(b, t // tm),
        in_specs=[
            pl.BlockSpec((1, CONV_HALO, c), lambda bi, i: (bi, jnp.maximum(i * per - 1, 0), 0)),
            pl.BlockSpec((1, tm, c), lambda bi, i: (bi, i, 0)),
            const((CONV_W, SUBLANES, c)), const((1, c)), const((1, c)), const((1, c)),
        ],
        out_specs=pl.BlockSpec((1, tm, c), lambda bi, i: (bi, i, 0)),
        out_shape=jax.ShapeDtypeStruct((b, t, c), BF16),
        scratch_shapes=[pltpu.VMEM((SUBLANES, tm + CONV_HALO, c), F32), pltpu.VMEM((tm, c), F32)],
        compiler_params=_params(("arbitrary", "arbitrary")),
    )(u, u, w_rows, bias.reshape(1, c), g.reshape(1, c), beta.reshape(1, c))


def _conv_sample_kernel(state_ref, u_ref, w_ref, b_ref, g_ref, beta_ref, o_ref):
    hist = state_ref.shape[0]
    for t in range(u_ref.shape[0]):
        acc = jnp.zeros(u_ref.shape[1:], F32) + b_ref[...]
        for j in range(CONV_W):
            src = t + j
            row = state_ref[src] if src < hist else u_ref[src - hist]
            acc = acc + row * w_ref[j:j + 1, :]
        o_ref[t] = _ln_silu(acc, g_ref[...], beta_ref[...])


def _conv_sample(state, u, w, bias, g, beta):
    s, b, c = u.shape
    return pl.pallas_call(
        _conv_sample_kernel,
        name="conv_sample",
        out_shape=jax.ShapeDtypeStruct((s, b, c), F32),
        compiler_params=pltpu.CompilerParams(vmem_limit_bytes=VMEM_LIMIT),
    )(state, u, w, bias.reshape(1, c), g.reshape(1, c), beta.reshape(1, c))


def _post_kernel(x_ref, a_ref, c_ref, p_ref, wa_ref, wc_ref, g_ref, wg_ref, wu_ref, wo_ref,
                 gp_ref, wpg_ref, wpp_ref, o_ref):
    x = x_ref[...] + _dot(a_ref[...], wa_ref[...]) + _dot(c_ref[...], wc_ref[...])
    x = _swiglu_residual(x, g_ref, wg_ref, wu_ref, wo_ref)
    gate = _sigmoid(_dot(_rms(x, gp_ref[...]).astype(BF16), wpg_ref[...]))
    o_ref[...] = x + gate * _dot(p_ref[...].astype(BF16), wpp_ref[...])


def _post(x, attn, conv, p, lw, tm):
    m, d = x.shape
    dff = lw["ffn2_w_out"].shape[1]
    assert dff % MXU_WIDTH == 0
    row = lambda w: pl.BlockSpec((tm, w), lambda i: (i, 0))
    return pl.pallas_call(
        _post_kernel,
        name="post",
        grid=(m // tm,),
        in_specs=[row(d), row(attn.shape[1]), row(conv.shape[1]), row(p.shape[1]),
                  _resident(lw["wmo_a"].shape), _resident(lw["wmo_c"].shape)]
        + _ffn_specs(d, dff, lw["layer"])
        + [_resident((1, d)), _resident(lw["wg"].shape), _resident(lw["wp"].shape)],
        out_specs=row(d),
        out_shape=jax.ShapeDtypeStruct((m, d), F32),
        compiler_params=_params(("arbitrary",)),
    )(x, attn, conv, p, lw["wmo_a"], lw["wmo_c"], lw["ffn2_norm"].reshape(1, d), lw["ffn2_w_in"],
      lw["ffn2_w_in"], lw["ffn2_w_out"], lw["ple_norm"].reshape(1, d), lw["wg"], lw["wp"])


def _sample_attn_kernel(n_pages, pt_ref, sb_ref, q_ref, kn_ref, vn_ref, lfn_ref, ck_hbm, cv_hbm, clf_hbm,
                        o_ref, kbuf, vbuf, lfbuf, sem, qbd_ref, m_ref, l_ref, acc_ref, r_ref):
    bi = pl.program_id(0)
    npg = PAGES_PER_GROUP
    n_groups = n_pages // npg
    s_q = q_ref.shape[1]
    rows = s_q * N_HEADS
    page = kn_ref.shape[2]

    def group_copies(g, slot):
        copies = []
        for pi in range(npg):
            row = pt_ref[bi * n_pages + (n_pages - 1 - (g * npg + pi))]
            copies.append(pltpu.make_async_copy(ck_hbm.at[row], kbuf.at[slot, pi], sem.at[slot, 0]))
            copies.append(pltpu.make_async_copy(cv_hbm.at[row], vbuf.at[slot, pi], sem.at[slot, 1]))
            copies.append(pltpu.make_async_copy(clf_hbm.at[row], lfbuf.at[slot, pi], sem.at[slot, 2]))
        return copies

    for copy in group_copies(0, 0):
        copy.start()

    row_i = lax.broadcasted_iota(jnp.int32, (rows, ATT_W), 0)
    lane_i = lax.broadcasted_iota(jnp.int32, (rows, ATT_W), 1)
    own_head = (lane_i // HEAD_DIM) == (row_i % N_HEADS)
    q = q_ref[0]
    q_rep = jnp.broadcast_to(q[:, None, :], (s_q, N_HEADS, ATT_W)).reshape(rows, ATT_W)
    qbd_ref[...] = jnp.where(own_head, q_rep, 0.0).astype(BF16)
    m_ref[...] = jnp.full_like(m_ref, NEG)
    l_ref[...] = jnp.zeros_like(l_ref)
    acc_ref[...] = jnp.zeros_like(acc_ref)

    jj = lax.broadcasted_iota(jnp.int32, (page, 2 * page), 0)
    ss = lax.broadcasted_iota(jnp.int32, (page, 2 * page), 1)
    later_and_all = jnp.where(jnp.logical_or(jj > ss, ss >= page), 1.0, 0.0).astype(BF16)

    def forget_sums(lft_page):
        both = sum(_dot(part, later_and_all) for part in _split3(lft_page))
        return both[:, :page], both[:, page:]

    def logits(kt_page, bias):
        return _dot(qbd_ref[...], kt_page.astype(BF16)) + jnp.concatenate([bias] * s_q, axis=0)

    def attend(s, vt_pages):
        m_prev = m_ref[...]
        m_new = jnp.maximum(m_prev, jnp.max(s, axis=1, keepdims=True))
        p = jnp.exp(s - m_new).astype(BF16)
        alpha = jnp.exp(m_prev - m_new)
        l_ref[...] = alpha * l_ref[...] + jnp.sum(p.astype(F32), axis=1, keepdims=True)
        pv = sum(_dot_nt(p[:, n * page:(n + 1) * page], vt.astype(BF16)) for n, vt in enumerate(vt_pages))
        acc_ref[...] = alpha * acc_ref[...] + pv
        m_ref[...] = m_new

    suffix, total = forget_sums(lfn_ref[0])
    s_new = logits(kn_ref[0], suffix)
    key_i = lax.broadcasted_iota(jnp.int32, s_new.shape, 1)
    q_i = lax.broadcasted_iota(jnp.int32, s_new.shape, 0) // N_HEADS
    attend(jnp.where(key_i <= q_i, s_new, NEG), [vn_ref[0]])
    r_ref[...] = total

    def body(state):
        g, _ = state
        slot = g % 2
        for copy in group_copies(g, slot):
            copy.wait()
        run = r_ref[...]
        biases = []
        for pi in range(npg):
            suffix, total = forget_sums(lfbuf[slot, pi])
            biases.append(suffix + run)
            run = run + total
        r_ref[...] = run
        reach = sb_ref[0] + jnp.concatenate([run + 1e-5 * jnp.abs(run)] * s_q, axis=0) - m_ref[...]
        more = jnp.logical_and(g + 1 < n_groups, jnp.max(reach) >= -SKIP_THRESH_BF16)

        @pl.when(more)
        def _():
            for copy in group_copies(g + 1, 1 - slot):
                copy.start()

        s = jnp.concatenate([logits(kbuf[slot, pi], biases[pi]) for pi in range(npg)], axis=1)
        attend(s, [vbuf[slot, pi] for pi in range(npg)])
        return g + 1, more.astype(jnp.int32)

    lax.while_loop(lambda state: state[1] > 0, body, (jnp.int32(0), jnp.int32(1)))

    out = jnp.where(own_head, acc_ref[...] / l_ref[...], 0.0)
    o_ref[0] = jnp.sum(out.reshape(s_q, N_HEADS, ATT_W), axis=1)


def _sample_attn(pt, s_bound, q, kt_new, vt_new, lft_new, cache_kt, cache_vt, cache_lft, n_pages):
    b, s_q, _ = q.shape
    page = cache_kt.shape[2]
    npg = PAGES_PER_GROUP
    assert n_pages % npg == 0

    def per_batch(shape):
        return pl.BlockSpec((1,) + shape, lambda bi, pt_ref, sb_ref: (bi, 0, 0))

    in_hbm = pl.BlockSpec(memory_space=pl.ANY)
    rows = s_q * N_HEADS
    grid_spec = pltpu.PrefetchScalarGridSpec(
        num_scalar_prefetch=2,
        grid=(b,),
        in_specs=[per_batch((s_q, ATT_W)), per_batch((ATT_W, page)), per_batch((ATT_W, page)),
                  per_batch((N_HEADS, page)), in_hbm, in_hbm, in_hbm],
        out_specs=per_batch((s_q, ATT_W)),
        scratch_shapes=[pltpu.VMEM((2, npg, ATT_W, page), F32), pltpu.VMEM((2, npg, ATT_W, page), F32),
                        pltpu.VMEM((2, npg, N_HEADS, page), F32), pltpu.SemaphoreType.DMA((2, 3)),
                        pltpu.VMEM((rows, ATT_W), BF16), pltpu.VMEM((rows, 1), F32),
                        pltpu.VMEM((rows, 1), F32), pltpu.VMEM((rows, ATT_W), F32),
                        pltpu.VMEM((N_HEADS, page), F32)],
    )
    return pl.pallas_call(
        functools.partial(_sample_attn_kernel, n_pages),
        name="attn_sample",
        grid_spec=grid_spec,
        out_shape=jax.ShapeDtypeStruct((b, s_q, ATT_W), F32),
        compiler_params=_params(("arbitrary",)),
    )(pt, s_bound, q, kt_new, vt_new, lft_new, cache_kt, cache_vt, cache_lft)


def _layer_weights(l, w):
    wmi = w["w_mix_in"][l]
    d = wmi.shape[0]
    n_qkv = 3 * ATT_W
    pad_rows = BF16_ROWS - N_HEADS
    wt = jnp.concatenate([wmi[:, ATT_W:n_qkv + N_HEADS].T, jnp.zeros((pad_rows, d), F32)], axis=0)
    wn = jnp.concatenate([wmi[:, :ATT_W], wmi[:, n_qkv + N_HEADS:]], axis=1)
    bfp = jnp.broadcast_to(w["b_forget"][l][:, None], (N_HEADS, LANES))
    kgt = jnp.broadcast_to(jnp.tile(w["k_norm"][l], N_HEADS)[:, None], (ATT_W, LANES))
    head_of = jnp.arange(ATT_W) // HEAD_DIM
    gmat = jnp.where(head_of[:, None] == head_of[None, :], 1.0 / HEAD_DIM, 0.0).astype(BF16)
    s_bound = (HEAD_DIM ** 0.5) * jnp.max(jnp.abs(w["q_norm"][l])) * jnp.max(jnp.abs(w["k_norm"][l])) * 1.02
    wmo = w["w_mix_out"][l].astype(BF16)
    return dict(
        layer=l, ffn1_norm=w["ffn1_norm"][l], ffn1_w_in=w["ffn1_w_in_bf"], ffn1_w_out=w["ffn1_w_out_bf"],
        mix_norm=w["mix_norm"][l], wt=wt.astype(BF16), wn=wn.astype(BF16), bfp=bfp,
        qg=jnp.tile(w["q_norm"][l], N_HEADS).reshape(1, ATT_W), kgt=kgt, gmat=gmat,
        s_bound=s_bound.reshape(1).astype(F32),
        conv_w=w["conv_w"][l], conv_b=w["conv_b"][l], conv_ln_g=w["conv_ln_g"][l],
        conv_ln_b=w["conv_ln_b"][l], wmo_a=wmo[:ATT_W], wmo_c=wmo[ATT_W:],
        ffn2_norm=w["ffn2_norm"][l], ffn2_w_in=w["ffn2_w_in_bf"], ffn2_w_out=w["ffn2_w_out_bf"],
        ple_norm=w["ple_norm"][l], wg=w["w_ple_gate"][l].astype(BF16), wp=w["w_ple_proj"][l].astype(BF16),
    )


def _mixin_call(x3, lw, tm):
    return _mixin(x3, lw["mix_norm"], lw["wt"], lw["wn"], lw["bfp"], lw["qg"], lw["kgt"], lw["gmat"], tm)


def kernel(x_prompt, x_sample, p_prompt, p_sample, cache_k, cache_v, cache_logf, state_conv, page_table, ffn1_norm, ffn1_w_in, ffn1_w_out, mix_norm, w_mix_in, b_forget, q_norm, k_norm, conv_w, conv_b, conv_ln_g, conv_ln_b, w_mix_out, ffn2_norm, ffn2_w_in, ffn2_w_out, ple_norm, w_ple_gate, w_ple_proj):
    depth = ffn1_norm.shape[0]
    b, t, d = x_prompt.shape
    sb, ss, _ = x_sample.shape
    n_pool, page = cache_k.shape[1], cache_k.shape[2]
    n_pages = page_table.shape[1]
    hist = state_conv.shape[2]
    ms = sb * ss

    w = dict(ffn1_norm=ffn1_norm, mix_norm=mix_norm, w_mix_in=w_mix_in, b_forget=b_forget, q_norm=q_norm,
             k_norm=k_norm, conv_w=conv_w, conv_b=conv_b, conv_ln_g=conv_ln_g, conv_ln_b=conv_ln_b,
             w_mix_out=w_mix_out, ffn2_norm=ffn2_norm, ple_norm=ple_norm, w_ple_gate=w_ple_gate,
             w_ple_proj=w_ple_proj,
             ffn1_w_in_bf=ffn1_w_in.astype(BF16), ffn1_w_out_bf=ffn1_w_out.astype(BF16),
             ffn2_w_in_bf=ffn2_w_in.astype(BF16), ffn2_w_out_bf=ffn2_w_out.astype(BF16))

    ckt = jnp.transpose(cache_k, (0, 1, 3, 4, 2)).reshape(depth * n_pool, ATT_W, page)
    cvt = jnp.transpose(cache_v, (0, 1, 3, 4, 2)).reshape(depth * n_pool, ATT_W, page)
    clft = jnp.transpose(cache_logf, (0, 1, 3, 2)).reshape(depth * n_pool, N_HEADS, page)
    state_t = jnp.transpose(state_conv, (0, 2, 1, 3))

    xp = x_prompt.reshape(b * t, d)
    xs = x_sample.reshape(ms, d)
    outs = {name: [] for name in ("kp", "vp", "fp", "cp", "ks", "vs", "fs", "cs")}
    heads_last = lambda a, lead: jnp.transpose(a.reshape(lead + (N_HEADS, HEAD_DIM, -1)),
                                               tuple(range(len(lead))) + (len(lead) + 2, len(lead), len(lead) + 1))

    for l in range(depth):
        lw = _layer_weights(l, w)

        xp = _ffn(xp, lw["ffn1_norm"], lw["ffn1_w_in"], lw["ffn1_w_out"], l, FFN_ROW_TILE)
        qb, ktb, vb, kt, vt, lft, ct, u = _mixin_call(xp.reshape(b, t, d), lw, ROW_TILE)
        attn = _attn_prompt(qb, ktb, vb, ct, lw["s_bound"], ATT_TILE)
        conv = _conv_prompt(u, lw["conv_w"], lw["conv_b"], lw["conv_ln_g"], lw["conv_ln_b"], ROW_TILE)
        xp = _post(xp, attn.reshape(b * t, ATT_W), conv.reshape(b * t, -1),
                   p_prompt[l].reshape(b * t, -1), lw, ROW_TILE)
        outs["kp"].append(heads_last(kt, (b,)))
        outs["vp"].append(heads_last(vt, (b,)))
        outs["fp"].append(jnp.transpose(lft, (0, 2, 1)))
        outs["cp"].append(u[:, t - hist:, :])

        xs = _ffn(xs, lw["ffn1_norm"], lw["ffn1_w_in"], lw["ffn1_w_out"], l, ms)
        qb, _, _, kt, vt, lft, _, u = _mixin_call(xs.reshape(1, ms, d), lw, ms)
        new_page = lambda a: jnp.pad(jnp.transpose(a[0].reshape(-1, sb, ss), (1, 0, 2)),
                                     ((0, 0), (0, 0), (0, page - ss)))
        pt = (page_table + l * n_pool).reshape(-1)
        attn = _sample_attn(pt, lw["s_bound"], qb.reshape(sb, ss, ATT_W).astype(F32), new_page(kt),
                            new_page(vt), new_page(lft), ckt, cvt, clft, n_pages)
        u_t = jnp.transpose(u.reshape(sb, ss, -1), (1, 0, 2))
        conv = _conv_sample(state_t[l], u_t, lw["conv_w"], lw["conv_b"], lw["conv_ln_g"], lw["conv_ln_b"])
        conv = jnp.transpose(conv, (1, 0, 2)).reshape(ms, -1)
        xs = _post(xs, attn.reshape(ms, ATT_W).astype(BF16), conv.astype(BF16),
                   p_sample[l].reshape(ms, -1), lw, ms)
        outs["ks"].append(kt[0].T.reshape(sb, ss, N_HEADS, HEAD_DIM))
        outs["vs"].append(vt[0].T.reshape(sb, ss, N_HEADS, HEAD_DIM))
        outs["fs"].append(lft[0].T.reshape(sb, ss, N_HEADS))
        outs["cs"].append(jnp.transpose(jnp.concatenate([state_t[l], u_t], axis=0)[ss:], (1, 0, 2)))

    stack = lambda name: jnp.stack(outs[name])
    return (xp.reshape(b, t, d), xs.reshape(sb, ss, d), stack("kp"), stack("vp"), stack("fp"), stack("cp"),
            stack("ks"), stack("vs"), stack("fs"), stack("cs"))
```

```python
import functools

import jax
import jax.numpy as jnp
from jax import lax
from jax.experimental import pallas as pl
from jax.experimental.pallas import tpu as pltpu

F32 = jnp.float32
BF16 = jnp.bfloat16

EPS = 1e-6
N_HEADS = 8
HEAD_DIM = 64
ATT_W = N_HEADS * HEAD_DIM
CONV_W = 31
LANES = 128
SUBLANES = 8
BF16_ROWS = 16
MXU_WIDTH = 256
V7X_VMEM_BYTES = 64 * 2 ** 20
VMEM_LIMIT = V7X_VMEM_BYTES - 8 * 2 ** 20
NEG = -1e30
SKIP_THRESH = 106.0
SKIP_THRESH_BF16 = 94.0
FIXED_REF_LIMIT = 60.0

ROW_TILE = 512
ATT_TILE = 512
CONV_HALO = 32
CONV_CHUNK = 32
PAGES_PER_GROUP = 8
FFN_ROW_TILE = 1024


def _params(sem):
    return pltpu.CompilerParams(dimension_semantics=sem, vmem_limit_bytes=VMEM_LIMIT)


def _rms(x, g):
    return x * lax.rsqrt(jnp.mean(x * x, axis=-1, keepdims=True) + EPS) * g


def _sigmoid(x):
    return 1.0 / (1.0 + jnp.exp(-x))


def _silu(x):
    return x * _sigmoid(x)


def _log_sigmoid(x):
    return -(jnp.maximum(-x, 0.0) + jnp.log1p(jnp.exp(-jnp.abs(x))))


def _dot(a, b):
    return jnp.dot(a, b, preferred_element_type=F32)


def _dot_nt(a, b):
    return lax.dot_general(a, b, (((1,), (1,)), ((), ())), preferred_element_type=F32)


def _split3(x):
    hi = x.astype(BF16)
    r = x - hi.astype(F32)
    mid = r.astype(BF16)
    lo = (r - mid.astype(F32)).astype(BF16)
    return hi, mid, lo


def _lane_tile(x, width):
    return jnp.concatenate([x] * (width // LANES), axis=1)


def _swiglu_residual(x, g_ref, wg_ref, wu_ref, wo_ref):
    h = _rms(x, g_ref[...]).astype(BF16)
    acc = None
    for c in range(wo_ref.shape[0] // MXU_WIDTH):
        cols = slice(c * MXU_WIDTH, (c + 1) * MXU_WIDTH)
        gate = _dot(h, wg_ref[:, cols])
        up = _dot(h, wu_ref[:, cols])
        part = _dot((_silu(gate) * up).astype(BF16), wo_ref[cols, :])
        acc = part if acc is None else acc + part
    return x + 0.5 * acc


def _ffn_kernel(x_ref, g_ref, wg_ref, wu_ref, wo_ref, o_ref):
    o_ref[...] = _swiglu_residual(x_ref[...], g_ref, wg_ref, wu_ref, wo_ref)


def _resident(shape, index=None):
    index = index or (0,) * len(shape)
    return pl.BlockSpec(shape, lambda i: index, pipeline_mode=pl.Buffered(1))


def _ffn_specs(d, dff, layer):
    return [_resident((1, d)), _resident((None, d, dff), (layer, 0, 0)),
            _resident((None, d, dff), (layer, 0, 1)), _resident((None, dff, d), (layer, 0, 0))]


def _ffn(x, g, w_in, w_out, layer, tm):
    m, d = x.shape
    dff = w_out.shape[1]
    assert dff % MXU_WIDTH == 0
    return pl.pallas_call(
        _ffn_kernel,
        name="ffn",
        grid=(m // tm,),
        in_specs=[pl.BlockSpec((tm, d), lambda i: (i, 0))] + _ffn_specs(d, dff, layer),
        out_specs=pl.BlockSpec((tm, d), lambda i: (i, 0)),
        out_shape=jax.ShapeDtypeStruct((m, d), F32),
        compiler_params=_params(("arbitrary",)),
    )(x, g.reshape(1, d), w_in, w_in, w_out)


def _lane_cumsum(x):
    n = x.shape[1]
    idx = lax.broadcasted_iota(jnp.int32, x.shape, 1)
    s = 1
    while s < n:
        x = x + jnp.where(idx >= s, pltpu.roll(x, s, axis=1), 0.0)
        s *= 2
    return x


def _mixin_kernel(x_ref, g_ref, wt_ref, wn_ref, bf_ref, qg_ref, kgt_ref, gmat_ref,
                  qb_ref, ktb_ref, vb_ref, kt_ref, vt_ref, lft_ref, ct_ref, u_ref, carry_ref):
    i = pl.program_id(1)
    tm = x_ref.shape[1]
    h = _rms(x_ref[0], g_ref[...]).astype(BF16)

    zt = _dot_nt(wt_ref[...], h)
    k3 = zt[:ATT_W].reshape(N_HEADS, HEAD_DIM, tm)
    ms = jnp.mean(k3 * k3, axis=1, keepdims=True)
    ktn = (k3 * lax.rsqrt(ms + EPS)).reshape(ATT_W, tm) * _lane_tile(kgt_ref[...], tm)
    kt_ref[0] = ktn
    ktb_ref[0] = ktn.astype(BF16)
    vt = zt[ATT_W:2 * ATT_W]
    vt_ref[0] = vt
    vb_ref[0] = vt.T.astype(BF16)

    lft = _log_sigmoid(zt[2 * ATT_W:2 * ATT_W + N_HEADS] + _lane_tile(bf_ref[...], tm))
    lft_ref[0] = lft

    @pl.when(i == 0)
    def _():
        carry_ref[...] = jnp.zeros_like(carry_ref)

    c = _lane_cumsum(lft) + carry_ref[:, :1]
    ct_ref[0] = c
    carry_ref[...] = jnp.broadcast_to(c[:, -1:], carry_ref.shape)

    z = _dot(h, wn_ref[...])
    q = z[:, :ATT_W]
    ms_q = _dot((q * q).astype(BF16), gmat_ref[...])
    qb_ref[0] = (q * lax.rsqrt(ms_q + EPS) * qg_ref[...] * (HEAD_DIM ** -0.5)).astype(BF16)
    cw = (z.shape[1] - ATT_W) // 2
    u_ref[0] = z[:, ATT_W:ATT_W + cw] * _sigmoid(z[:, ATT_W + cw:])


def _mixin(x, g, wt, wn, bfp, qg, kgt, gmat, tm):
    b, t, d = x.shape
    cw = (wn.shape[1] - ATT_W) // 2
    const = lambda shape: pl.BlockSpec(shape, lambda bi, i: (0,) * len(shape))
    row = lambda w: pl.BlockSpec((1, tm, w), lambda bi, i: (bi, i, 0))
    col = lambda r: pl.BlockSpec((1, r, tm), lambda bi, i: (bi, 0, i))
    return pl.pallas_call(
        _mixin_kernel,
        name="mixin",
        grid=(b, t // tm),
        in_specs=[row(d), const((1, d)), const(wt.shape), const(wn.shape), const(bfp.shape),
                  const(qg.shape), const(kgt.shape), const(gmat.shape)],
        out_specs=[row(ATT_W), col(ATT_W), row(ATT_W), col(ATT_W), col(ATT_W), col(N_HEADS),
                   col(N_HEADS), row(cw)],
        out_shape=[jax.ShapeDtypeStruct((b, t, ATT_W), BF16), jax.ShapeDtypeStruct((b, ATT_W, t), BF16),
                   jax.ShapeDtypeStruct((b, t, ATT_W), BF16), jax.ShapeDtypeStruct((b, ATT_W, t), F32),
                   jax.ShapeDtypeStruct((b, ATT_W, t), F32), jax.ShapeDtypeStruct((b, N_HEADS, t), F32),
                   jax.ShapeDtypeStruct((b, N_HEADS, t), F32), jax.ShapeDtypeStruct((b, t, cw), F32)],
        scratch_shapes=[pltpu.VMEM((N_HEADS, LANES), F32)],
        compiler_params=_params(("arbitrary", "arbitrary")),
    )(x, g.reshape(1, d), wt, wn, bfp, qg, kgt, gmat)


def _head_pair(q_ref):
    q = q_ref[0]
    first = lax.broadcasted_iota(jnp.int32, q.shape, 1) < HEAD_DIM
    zero = jnp.zeros_like(q)
    return (jnp.where(first, q, zero), jnp.where(first, zero, q)), first


def _attn_fixed_kernel(sb_ref, q_ref, kt_ref, v_ref, c2_ref, c8_ref, o_ref, acc_ref):
    hp = pl.program_id(1)
    i = pl.program_id(2)
    tq = q_ref.shape[1]
    q_heads, first = _head_pair(q_ref)
    q0 = pl.multiple_of(i * tq, tq)
    c_base = c2_ref[0, 0, :, pl.ds(q0, LANES)][:, :1]

    slab = jnp.concatenate([c8_ref[0, :, pl.ds(q0, tq)], jnp.zeros((LANES - N_HEADS, tq), F32)], axis=0)
    by_row = slab.T
    head_lane = lax.broadcasted_iota(jnp.int32, by_row.shape, 1)
    shift = []
    for hh in range(2):
        mine = jnp.sum(jnp.where(head_lane == 2 * hp + hh, by_row, 0.0), axis=1, keepdims=True)
        shift.append(mine - c_base[hh:hh + 1, :] - sb_ref[0])

    c_all = c2_ref[0, 0]
    col = lax.broadcasted_iota(jnp.int32, c_all.shape, 1)
    block_end = jnp.logical_and(col % tq == tq - 1, col < q0)
    alive = (c_base - c_all) + 1e-5 * jnp.abs(c_all) >= -SKIP_THRESH_BF16
    oldest = jnp.where(jnp.logical_and(block_end, alive), col // tq, i)
    reach = [i - jnp.min(oldest[hh:hh + 1, :]) for hh in range(2)]
    shared = jnp.minimum(reach[0], reach[1])

    one = jnp.ones((tq, LANES), BF16)

    def block(j):
        k0 = pl.multiple_of(j * tq, tq)
        kt = kt_ref[0, :, pl.ds(k0, tq)]
        vb = v_ref[0, pl.ds(k0, tq), :]
        ck = c2_ref[0, 0, :, pl.ds(k0, tq)] - c_base
        first_k = lax.broadcasted_iota(jnp.int32, vb.shape, 1) < HEAD_DIM
        v_aug = lambda hh: jnp.where(first_k, vb, one) if hh == 0 else jnp.where(first_k, one, vb)
        return kt, ck, v_aug

    def causal(x, row0):
        r = lax.broadcasted_iota(jnp.int32, x.shape, 0) + row0
        cidx = lax.broadcasted_iota(jnp.int32, x.shape, 1)
        return jnp.where(cidx <= r, x, NEG)

    half = tq // 2
    kt, ck, v_aug = block(i)
    for hh in range(2):
        values = v_aug(hh)
        x = (_dot(q_heads[hh][:half], kt[:, :half]) + shift[hh][:half]) - ck[hh:hh + 1, :half]
        acc_ref[hh, 0:half] = _dot(jnp.exp(causal(x, 0)).astype(BF16), values[:half])
        x = (_dot(q_heads[hh][half:], kt) + shift[hh][half:]) - ck[hh:hh + 1, :]
        acc_ref[hh, half:tq] = _dot(jnp.exp(causal(x, half)).astype(BF16), values)

    def step(j, heads):
        kt, ck, v_aug = block(j)
        for hh in heads:
            x = (_dot(q_heads[hh], kt) + shift[hh]) - ck[hh:hh + 1, :]
            acc_ref[hh] += _dot(jnp.exp(x).astype(BF16), v_aug(hh))

    def sweep(start, stop, heads):
        def body(n, carry):
            step(i - 1 - n, heads)
            return carry
        lax.fori_loop(start, stop, body, 0)

    sweep(0, shared, (0, 1))
    sweep(shared, reach[0], (0,))
    sweep(shared, reach[1], (1,))

    a0, a1 = acc_ref[0], acc_ref[1]
    out = jnp.where(first, a0 / pltpu.roll(a0, HEAD_DIM, axis=1), a1 / pltpu.roll(a1, HEAD_DIM, axis=1))
    o_ref[0] = out.astype(o_ref.dtype)


def _attn_online_kernel(sb_ref, q_ref, kt_ref, v_ref, c2_ref, c8_ref, o_ref, m_ref, l_ref, acc_ref):
    del c8_ref
    i = pl.program_id(2)
    tq = q_ref.shape[1]
    q_heads, first = _head_pair(q_ref)
    q0 = pl.multiple_of(i * tq, tq)
    c_base = c2_ref[0, 0, :, pl.ds(q0, LANES)][:, :1]
    s_bound = sb_ref[0]

    m_ref[...] = jnp.full_like(m_ref, NEG)
    l_ref[...] = jnp.zeros_like(l_ref)
    acc_ref[...] = jnp.zeros_like(acc_ref)

    def step(j, diagonal):
        k0 = pl.multiple_of(j * tq, tq)
        kt = kt_ref[0, :, pl.ds(k0, tq)]
        vb = v_ref[0, pl.ds(k0, tq), :]
        ck = c2_ref[0, 0, :, pl.ds(k0, tq)] - c_base
        for hh in range(2):
            s = _dot(q_heads[hh], kt) - ck[hh:hh + 1, :]
            if diagonal:
                r = lax.broadcasted_iota(jnp.int32, s.shape, 0)
                cidx = lax.broadcasted_iota(jnp.int32, s.shape, 1)
                s = jnp.where(cidx <= r, s, NEG)
            m_prev = m_ref[hh]
            m_new = jnp.maximum(m_prev, jnp.max(s, axis=1, keepdims=True))
            p = jnp.exp(s - m_new)
            alpha = jnp.exp(m_prev - m_new)
            l_ref[hh] = alpha * l_ref[hh] + jnp.sum(p, axis=1, keepdims=True)
            acc_ref[hh] = alpha * acc_ref[hh] + _dot(p.astype(BF16), vb)
            m_ref[hh] = m_new

    def needed(j):
        k_last = pl.multiple_of(j * tq, tq) + tq - LANES
        c_last = c2_ref[0, 0, :, pl.ds(k_last, LANES)][:, LANES - 1:]
        best = s_bound - (c_last - c_base) + 1e-5 * jnp.abs(c_last)
        worst = NEG
        for hh in range(2):
            worst = jnp.maximum(worst, jnp.max(best[hh:hh + 1, :] - jnp.min(m_ref[hh])))
        return worst > -SKIP_THRESH

    step(i, True)

    def wanted(j):
        return jnp.logical_and(j >= 0, needed(jnp.maximum(j, 0))).astype(jnp.int32)

    def body(state):
        j, _ = state
        step(j, False)
        return j - 1, wanted(j - 1)

    lax.while_loop(lambda state: state[1] > 0, body, (i - 1, wanted(i - 1)))

    out = jnp.where(first, acc_ref[0] / l_ref[0], acc_ref[1] / l_ref[1])
    o_ref[0] = out.astype(o_ref.dtype)


def _attn_prompt(qb, ktb, vb, ct, s_bound, tq):
    b, t, _ = qb.shape
    pairs = N_HEADS // 2
    c4 = ct.reshape(b, pairs, 2, t)

    def call(body, name, scratch):
        grid_spec = pltpu.PrefetchScalarGridSpec(
            num_scalar_prefetch=1,
            grid=(b, pairs, t // tq),
            in_specs=[
                pl.BlockSpec((1, tq, LANES), lambda bi, hp, i, sb: (bi, i, hp)),
                pl.BlockSpec((1, LANES, t), lambda bi, hp, i, sb: (bi, hp, 0)),
                pl.BlockSpec((1, t, LANES), lambda bi, hp, i, sb: (bi, 0, hp)),
                pl.BlockSpec((1, 1, 2, t), lambda bi, hp, i, sb: (bi, hp, 0, 0)),
                pl.BlockSpec((1, N_HEADS, t), lambda bi, hp, i, sb: (bi, 0, 0)),
            ],
            out_specs=pl.BlockSpec((1, tq, LANES), lambda bi, hp, i, sb: (bi, i, hp)),
            scratch_shapes=scratch,
        )
        return pl.pallas_call(
            body,
            name=name,
            grid_spec=grid_spec,
            out_shape=jax.ShapeDtypeStruct((b, t, ATT_W), BF16),
            compiler_params=_params(("arbitrary", "arbitrary", "arbitrary")),
        )(s_bound, qb, ktb, vb, c4, ct)

    acc = pltpu.VMEM((2, tq, LANES), F32)
    stat = pltpu.VMEM((2, tq, 1), F32)
    return lax.cond(
        2.0 * s_bound[0] <= FIXED_REF_LIMIT,
        lambda: call(_attn_fixed_kernel, "attn_prompt", [acc]),
        lambda: call(_attn_online_kernel, "attn_prompt_online", [stat, stat, acc]),
    )


def _ln_silu(y, g, beta):
    mu = jnp.mean(y, axis=-1, keepdims=True)
    yc = y - mu
    var = jnp.mean(yc * yc, axis=-1, keepdims=True)
    return _silu(yc * lax.rsqrt(var + EPS) * g + beta)


def _conv_kernel(prev_ref, cur_ref, w_ref, b_ref, g_ref, beta_ref, o_ref, ext_ref, y_ref):
    i = pl.program_id(1)
    tm = cur_ref.shape[1]
    halo = prev_ref[0]
    ext_ref[0, 0:CONV_HALO, :] = jnp.where(i == 0, jnp.zeros_like(halo), halo)
    ext_ref[0, CONV_HALO:, :] = cur_ref[0]
    span = tm + CONV_HALO - SUBLANES
    for s in range(1, SUBLANES):
        ext_ref[s, 0:span, :] = ext_ref[0, s:s + span, :]
    first = CONV_HALO - (CONV_W - 1)

    def chunk(ci, carry):
        r0 = pl.multiple_of(ci * CONV_CHUNK, CONV_CHUNK)
        acc = jnp.zeros((CONV_CHUNK, cur_ref.shape[2]), F32) + b_ref[...]
        for j in range(CONV_W):
            whole, s = divmod(first + j, SUBLANES)
            w_rows = jnp.concatenate([w_ref[j]] * (CONV_CHUNK // SUBLANES), axis=0)
            acc = acc + ext_ref[s, pl.ds(r0 + SUBLANES * whole, CONV_CHUNK), :] * w_rows
        y_ref[pl.ds(r0, CONV_CHUNK), :] = acc
        return carry

    lax.fori_loop(0, tm // CONV_CHUNK, chunk, 0)
    o_ref[0] = _ln_silu(y_ref[...], g_ref[...], beta_ref[...]).astype(o_ref.dtype)


def _conv_prompt(u, w, bias, g, beta, tm):
    b, t, c = u.shape
    per = tm // CONV_HALO
    const = lambda shape: pl.BlockSpec(shape, lambda bi, i: (0,) * len(shape))
    w_rows = jnp.broadcast_to(w[:, None, :], (CONV_W, SUBLANES, c))
    return pl.pallas_call(
        _conv_kernel,
        name="conv_prompt",
        grid=(b, t // tm),
        in_specs=[
            pl.BlockSpec((1, CONV_HALO, c), lambda bi, i: (bi, jnp.maximum(i * per - 1, 0), 0)),
            pl.BlockSpec((1, tm, c), lambda bi, i: (bi, i, 0)),
            const((CONV_W, SUBLANES, c)), const((1, c)), const((1, c)), const((1, c)),
        ],
        out_specs=pl.BlockSpec((1, tm, c), lambda bi, i: (bi, i, 0)),
        out_shape=jax.ShapeDtypeStruct((b, t, c), BF16),
        scratch_shapes=[pltpu.VMEM((SUBLANES, tm + CONV_HALO, c), F32), pltpu.VMEM((tm, c), F32)],
        compiler_params=_params(("arbitrary", "arbitrary")),
    )(u, u, w_rows, bias.reshape(1, c), g.reshape(1, c), beta.reshape(1, c))


def _conv_sample_kernel(state_ref, u_ref, w_ref, b_ref, g_ref, beta_ref, o_ref):
    hist = state_ref.shape[0]
    for t in range(u_ref.shape[0]):
        acc = jnp.zeros(u_ref.shape[1:], F32) + b_ref[...]
        for j in range(CONV_W):
            src = t + j
            row = state_ref[src] if src < hist else u_ref[src - hist]
            acc = acc + row * w_ref[j:j + 1, :]
        o_ref[t] = _ln_silu(acc, g_ref[...], beta_ref[...])


def _conv_sample(state, u, w, bias, g, beta):
    s, b, c = u.shape
    return pl.pallas_call(
        _conv_sample_kernel,
        name="conv_sample",
        out_shape=jax.ShapeDtypeStruct((s, b, c), F32),
        compiler_params=pltpu.CompilerParams(vmem_limit_bytes=VMEM_LIMIT),
    )(state, u, w, bias.reshape(1, c), g.reshape(1, c), beta.reshape(1, c))


def _post_kernel(x_ref, a_ref, c_ref, p_ref, wa_ref, wc_ref, g_ref, wg_ref, wu_ref, wo_ref,
                 gp_ref, wpg_ref, wpp_ref, o_ref):
    x = x_ref[...] + _dot(a_ref[...], wa_ref[...]) + _dot(c_ref[...], wc_ref[...])
    x = _swiglu_residual(x, g_ref, wg_ref, wu_ref, wo_ref)
    gate = _sigmoid(_dot(_rms(x, gp_ref[...]).astype(BF16), wpg_ref[...]))
    o_ref[...] = x + gate * _dot(p_ref[...].astype(BF16), wpp_ref[...])


def _post(x, attn, conv, p, lw, tm):
    m, d = x.shape
    dff = lw["ffn2_w_out"].shape[1]
    assert dff % MXU_WIDTH == 0
    row = lambda w: pl.BlockSpec((tm, w), lambda i: (i, 0))
    return pl.pallas_call(
        _post_kernel,
        name="post",
        grid=(m // tm,),
        in_specs=[row(d), row(attn.shape[1]), row(conv.shape[1]), row(p.shape[1]),
                  _resident(lw["wmo_a"].shape), _resident(lw["wmo_c"].shape)]
        + _ffn_specs(d, dff, lw["layer"])
        + [_resident((1, d)), _resident(lw["wg"].shape), _resident(lw["wp"].shape)],
        out_specs=row(d),
        out_shape=jax.ShapeDtypeStruct((m, d), F32),
        compiler_params=_params(("arbitrary",)),
    )(x, attn, conv, p, lw["wmo_a"], lw["wmo_c"], lw["ffn2_norm"].reshape(1, d), lw["ffn2_w_in"],
      lw["ffn2_w_in"], lw["ffn2_w_out"], lw["ple_norm"].reshape(1, d), lw["wg"], lw["wp"])


def _sample_attn_kernel(n_pages, pt_ref, sb_ref, q_ref, kn_ref, vn_ref, lfn_ref, ck_hbm, cv_hbm, clf_hbm,
                        o_ref, kbuf, vbuf, lfbuf, sem, first_slot_ref, qbd_ref, m_ref, l_ref, acc_ref, r_ref):
    bi = pl.program_id(0)
    n_batch = pl.num_programs(0)
    npg = PAGES_PER_GROUP
    n_groups = n_pages // npg
    s_q = q_ref.shape[1]
    rows = s_q * N_HEADS
    page = kbuf.shape[3]
    n_new = kn_ref.shape[2]

    def group_copies(batch, g, slot):
        copies = []
        for pi in range(npg):
            row = pt_ref[batch * n_pages + (n_pages - 1 - (g * npg + pi))]
            copies.append(pltpu.make_async_copy(ck_hbm.at[row], kbuf.at[slot, pi], sem.at[slot, 0]))
            copies.append(pltpu.make_async_copy(cv_hbm.at[row], vbuf.at[slot, pi], sem.at[slot, 1]))
            copies.append(pltpu.make_async_copy(clf_hbm.at[row], lfbuf.at[slot, pi], sem.at[slot, 2]))
        return copies

    @pl.when(bi == 0)
    def _():
        first_slot_ref[0] = 0
        for copy in group_copies(0, 0, 0):
            copy.start()

    first_slot = first_slot_ref[0]

    row_i = lax.broadcasted_iota(jnp.int32, (rows, ATT_W), 0)
    lane_i = lax.broadcasted_iota(jnp.int32, (rows, ATT_W), 1)
    own_head = (lane_i // HEAD_DIM) == (row_i % N_HEADS)
    q = q_ref[0]
    q_rep = jnp.broadcast_to(q[:, None, :], (s_q, N_HEADS, ATT_W)).reshape(rows, ATT_W)
    qbd_ref[...] = jnp.where(own_head, q_rep, 0.0).astype(BF16)
    m_ref[...] = jnp.full_like(m_ref, NEG)
    l_ref[...] = jnp.zeros_like(l_ref)
    acc_ref[...] = jnp.zeros_like(acc_ref)

    jj = lax.broadcasted_iota(jnp.int32, (page, 2 * page), 0)
    ss = lax.broadcasted_iota(jnp.int32, (page, 2 * page), 1)
    later_and_all = jnp.where(jnp.logical_or(jj > ss, ss >= page), 1.0, 0.0).astype(BF16)

    def forget_sums(lft_keys, weights):
        both = sum(_dot(part, weights) for part in _split3(lft_keys))
        half = weights.shape[1] // 2
        return both[:, :half], both[:, half:]

    def logits(kt_keys, bias):
        return _dot(qbd_ref[...], kt_keys.astype(BF16)) + jnp.concatenate([bias] * s_q, axis=0)

    def attend(s, vt_pages):
        m_prev = m_ref[...]
        m_new = jnp.maximum(m_prev, jnp.max(s, axis=1, keepdims=True))
        p = jnp.exp(s - m_new).astype(BF16)
        alpha = jnp.exp(m_prev - m_new)
        l_ref[...] = alpha * l_ref[...] + jnp.sum(p.astype(F32), axis=1, keepdims=True)
        pv, k0 = 0.0, 0
        for vt in vt_pages:
            pv = pv + _dot_nt(p[:, k0:k0 + vt.shape[1]], vt.astype(BF16))
            k0 += vt.shape[1]
        acc_ref[...] = alpha * acc_ref[...] + pv
        m_ref[...] = m_new

    jn = lax.broadcasted_iota(jnp.int32, (n_new, 2 * n_new), 0)
    sn = lax.broadcasted_iota(jnp.int32, (n_new, 2 * n_new), 1)
    same_row_later = jnp.logical_and(jn > sn, jn // s_q == sn // s_q)
    mine = jnp.logical_and(sn >= n_new, jn // s_q == bi)
    suffix, total = forget_sums(lfn_ref[0], jnp.where(jnp.logical_or(same_row_later, mine), 1.0, 0.0).astype(BF16))
    s_new = logits(kn_ref[0], suffix)
    key_i = lax.broadcasted_iota(jnp.int32, s_new.shape, 1)
    q_i = lax.broadcasted_iota(jnp.int32, s_new.shape, 0) // N_HEADS
    admitted = jnp.logical_and(key_i // s_q == bi, key_i % s_q <= q_i)
    attend(jnp.where(admitted, s_new, NEG), [vn_ref[0]])
    r_ref[...] = total

    def body(state):
        g, _ = state
        slot = (first_slot + g) % 2
        for copy in group_copies(bi, g, slot):
            copy.wait()
        run = r_ref[...]
        biases = []
        for pi in range(npg):
            suffix, total = forget_sums(lfbuf[slot, pi], later_and_all)
            biases.append(suffix + run)
            run = run + total
        r_ref[...] = run
        reach = sb_ref[0] + jnp.concatenate([run + 1e-5 * jnp.abs(run)] * s_q, axis=0) - m_ref[...]
        more = jnp.logical_and(g + 1 < n_groups, jnp.max(reach) >= -SKIP_THRESH_BF16)

        @pl.when(more)
        def _():
            for copy in group_copies(bi, g + 1, 1 - slot):
                copy.start()

        @pl.when(jnp.logical_and(jnp.logical_not(more), bi + 1 < n_batch))
        def _():
            for copy in group_copies(bi + 1, 0, 1 - slot):
                copy.start()

        s = jnp.concatenate([logits(kbuf[slot, pi], biases[pi]) for pi in range(npg)], axis=1)
        attend(s, [vbuf[slot, pi] for pi in range(npg)])
        return g + 1, more.astype(jnp.int32)

    done, _ = lax.while_loop(lambda state: state[1] > 0, body, (jnp.int32(0), jnp.int32(1)))
    first_slot_ref[0] = (first_slot + done) % 2

    out = jnp.where(own_head, acc_ref[...] / l_ref[...], 0.0)
    o_ref[0] = jnp.sum(out.reshape(s_q, N_HEADS, ATT_W), axis=1)


def _sample_attn(pt, s_bound, q, kt_new, vt_new, lft_new, cache_kt, cache_vt, cache_lft, n_pages):
    b, s_q, _ = q.shape
    page = cache_kt.shape[2]
    n_new = kt_new.shape[2]
    npg = PAGES_PER_GROUP
    assert n_pages % npg == 0 and n_new == page

    def per_batch(shape):
        return pl.BlockSpec((1,) + shape, lambda bi, pt_ref, sb_ref: (bi, 0, 0))

    def shared(shape):
        return pl.BlockSpec((1,) + shape, lambda bi, pt_ref, sb_ref: (0, 0, 0))

    in_hbm = pl.BlockSpec(memory_space=pl.ANY)
    rows = s_q * N_HEADS
    grid_spec = pltpu.PrefetchScalarGridSpec(
        num_scalar_prefetch=2,
        grid=(b,),
        in_specs=[per_batch((s_q, ATT_W)), shared((ATT_W, n_new)), shared((ATT_W, n_new)),
                  shared((N_HEADS, n_new)), in_hbm, in_hbm, in_hbm],
        out_specs=per_batch((s_q, ATT_W)),
        scratch_shapes=[pltpu.VMEM((2, npg, ATT_W, page), F32), pltpu.VMEM((2, npg, ATT_W, page), F32),
                        pltpu.VMEM((2, npg, N_HEADS, page), F32), pltpu.SemaphoreType.DMA((2, 3)),
                        pltpu.SMEM((1,), jnp.int32),
                        pltpu.VMEM((rows, ATT_W), BF16), pltpu.VMEM((rows, 1), F32),
                        pltpu.VMEM((rows, 1), F32), pltpu.VMEM((rows, ATT_W), F32),
                        pltpu.VMEM((N_HEADS, page), F32)],
    )
    return pl.pallas_call(
        functools.partial(_sample_attn_kernel, n_pages),
        name="attn_sample",
        grid_spec=grid_spec,
        out_shape=jax.ShapeDtypeStruct((b, s_q, ATT_W), F32),
        compiler_params=_params(("arbitrary",)),
    )(pt, s_bound, q, kt_new, vt_new, lft_new, cache_kt, cache_vt, cache_lft)


def _layer_weights(l, w):
    wmi = w["w_mix_in"][l]
    d = wmi.shape[0]
    n_qkv = 3 * ATT_W
    pad_rows = BF16_ROWS - N_HEADS
    wt = jnp.concatenate([wmi[:, ATT_W:n_qkv + N_HEADS].T, jnp.zeros((pad_rows, d), F32)], axis=0)
    wn = jnp.concatenate([wmi[:, :ATT_W], wmi[:, n_qkv + N_HEADS:]], axis=1)
    bfp = jnp.broadcast_to(w["b_forget"][l][:, None], (N_HEADS, LANES))
    kgt = jnp.broadcast_to(jnp.tile(w["k_norm"][l], N_HEADS)[:, None], (ATT_W, LANES))
    head_of = jnp.arange(ATT_W) // HEAD_DIM
    gmat = jnp.where(head_of[:, None] == head_of[None, :], 1.0 / HEAD_DIM, 0.0).astype(BF16)
    s_bound = (HEAD_DIM ** 0.5) * jnp.max(jnp.abs(w["q_norm"][l])) * jnp.max(jnp.abs(w["k_norm"][l])) * 1.02
    wmo = w["w_mix_out"][l].astype(BF16)
    return dict(
        layer=l, ffn1_norm=w["ffn1_norm"][l], ffn1_w_in=w["ffn1_w_in_bf"], ffn1_w_out=w["ffn1_w_out_bf"],
        mix_norm=w["mix_norm"][l], wt=wt.astype(BF16), wn=wn.astype(BF16), bfp=bfp,
        qg=jnp.tile(w["q_norm"][l], N_HEADS).reshape(1, ATT_W), kgt=kgt, gmat=gmat,
        s_bound=s_bound.reshape(1).astype(F32),
        conv_w=w["conv_w"][l], conv_b=w["conv_b"][l], conv_ln_g=w["conv_ln_g"][l],
        conv_ln_b=w["conv_ln_b"][l], wmo_a=wmo[:ATT_W], wmo_c=wmo[ATT_W:],
        ffn2_norm=w["ffn2_norm"][l], ffn2_w_in=w["ffn2_w_in_bf"], ffn2_w_out=w["ffn2_w_out_bf"],
        ple_norm=w["ple_norm"][l], wg=w["w_ple_gate"][l].astype(BF16), wp=w["w_ple_proj"][l].astype(BF16),
    )


def _mixin_call(x3, lw, tm):
    return _mixin(x3, lw["mix_norm"], lw["wt"], lw["wn"], lw["bfp"], lw["qg"], lw["kgt"], lw["gmat"], tm)


def kernel(x_prompt, x_sample, p_prompt, p_sample, cache_k, cache_v, cache_logf, state_conv, page_table, ffn1_norm, ffn1_w_in, ffn1_w_out, mix_norm, w_mix_in, b_forget, q_norm, k_norm, conv_w, conv_b, conv_ln_g, conv_ln_b, w_mix_out, ffn2_norm, ffn2_w_in, ffn2_w_out, ple_norm, w_ple_gate, w_ple_proj):
    depth = ffn1_norm.shape[0]
    b, t, d = x_prompt.shape
    sb, ss, _ = x_sample.shape
    n_pool, page = cache_k.shape[1], cache_k.shape[2]
    n_pages = page_table.shape[1]
    hist = state_conv.shape[2]
    ms = sb * ss

    w = dict(ffn1_norm=ffn1_norm, mix_norm=mix_norm, w_mix_in=w_mix_in, b_forget=b_forget, q_norm=q_norm,
             k_norm=k_norm, conv_w=conv_w, conv_b=conv_b, conv_ln_g=conv_ln_g, conv_ln_b=conv_ln_b,
             w_mix_out=w_mix_out, ffn2_norm=ffn2_norm, ple_norm=ple_norm, w_ple_gate=w_ple_gate,
             w_ple_proj=w_ple_proj,
             ffn1_w_in_bf=ffn1_w_in.astype(BF16), ffn1_w_out_bf=ffn1_w_out.astype(BF16),
             ffn2_w_in_bf=ffn2_w_in.astype(BF16), ffn2_w_out_bf=ffn2_w_out.astype(BF16))

    ckt = jnp.transpose(cache_k, (0, 1, 3, 4, 2)).reshape(depth * n_pool, ATT_W, page)
    cvt = jnp.transpose(cache_v, (0, 1, 3, 4, 2)).reshape(depth * n_pool, ATT_W, page)
    clft = jnp.transpose(cache_logf, (0, 1, 3, 2)).reshape(depth * n_pool, N_HEADS, page)
    state_t = jnp.transpose(state_conv, (0, 2, 1, 3))

    xp = x_prompt.reshape(b * t, d)
    xs = x_sample.reshape(ms, d)
    outs = {name: [] for name in ("kp", "vp", "fp", "cp", "ks", "vs", "fs", "cs")}
    heads_last = lambda a, lead: jnp.transpose(a.reshape(lead + (N_HEADS, HEAD_DIM, -1)),
                                               tuple(range(len(lead))) + (len(lead) + 2, len(lead), len(lead) + 1))

    for l in range(depth):
        lw = _layer_weights(l, w)

        xp = _ffn(xp, lw["ffn1_norm"], lw["ffn1_w_in"], lw["ffn1_w_out"], l, FFN_ROW_TILE)
        qb, ktb, vb, kt, vt, lft, ct, u = _mixin_call(xp.reshape(b, t, d), lw, ROW_TILE)
        attn = _attn_prompt(qb, ktb, vb, ct, lw["s_bound"], ATT_TILE)
        conv = _conv_prompt(u, lw["conv_w"], lw["conv_b"], lw["conv_ln_g"], lw["conv_ln_b"], ROW_TILE)
        xp = _post(xp, attn.reshape(b * t, ATT_W), conv.reshape(b * t, -1),
                   p_prompt[l].reshape(b * t, -1), lw, ROW_TILE)
        outs["kp"].append(heads_last(kt, (b,)))
        outs["vp"].append(heads_last(vt, (b,)))
        outs["fp"].append(jnp.transpose(lft, (0, 2, 1)))
        outs["cp"].append(u[:, t - hist:, :])

        xs = _ffn(xs, lw["ffn1_norm"], lw["ffn1_w_in"], lw["ffn1_w_out"], l, ms)
        qb, _, _, kt, vt, lft, _, u = _mixin_call(xs.reshape(1, ms, d), lw, ms)
        pt = (page_table + l * n_pool).reshape(-1)
        attn = _sample_attn(pt, lw["s_bound"], qb.reshape(sb, ss, ATT_W).astype(F32), kt, vt, lft,
                            ckt, cvt, clft, n_pages)
        u_t = jnp.transpose(u.reshape(sb, ss, -1), (1, 0, 2))
        conv = _conv_sample(state_t[l], u_t, lw["conv_w"], lw["conv_b"], lw["conv_ln_g"], lw["conv_ln_b"])
        conv = jnp.transpose(conv, (1, 0, 2)).reshape(ms, -1)
        xs = _post(xs, attn.reshape(ms, ATT_W).astype(BF16), conv.astype(BF16),
                   p_sample[l].reshape(ms, -1), lw, ms)
        outs["ks"].append(kt[0].T.reshape(sb, ss, N_HEADS, HEAD_DIM))
        outs["vs"].append(vt[0].T.reshape(sb, ss, N_HEADS, HEAD_DIM))
        outs["fs"].append(lft[0].T.reshape(sb, ss, N_HEADS))
        outs["cs"].append(jnp.transpose(jnp.concatenate([state_t[l], u_t], axis=0)[ss:], (1, 0, 2)))

    stack = lambda name: jnp.stack(outs[name])
    return (xp.reshape(b, t, d), xs.reshape(sb, ss, d), stack("kp"), stack("vp"), stack("fp"), stack("cp"),
            stack("ks"), stack("vs"), stack("fs"), stack("cs"))
```

```python
import functools

import jax
import jax.numpy as jnp
from jax import lax
from jax.experimental import pallas as pl
from jax.experimental.pallas import tpu as pltpu

F32 = jnp.float32
BF16 = jnp.bfloat16

EPS = 1e-6
N_HEADS = 8
HEAD_DIM = 64
ATT_W = N_HEADS * HEAD_DIM
CONV_W = 31
LANES = 128
SUBLANES = 8
BF16_ROWS = 16
MXU_WIDTH = 256
V7X_VMEM_BYTES = 64 * 2 ** 20
VMEM_LIMIT = V7X_VMEM_BYTES - 8 * 2 ** 20
NEG = -1e30
SKIP_THRESH = 106.0
SKIP_THRESH_BF16 = 94.0
FIXED_REF_LIMIT = 60.0

ROW_TILE = 512
ATT_TILE = 512
CONV_HALO = 32
CONV_CHUNK = 32
PAGES_PER_GROUP = 8
FFN_ROW_TILE = 1024


def _params(sem):
    return pltpu.CompilerParams(dimension_semantics=sem, vmem_limit_bytes=VMEM_LIMIT)


def _rms(x, g):
    return x * lax.rsqrt(jnp.mean(x * x, axis=-1, keepdims=True) + EPS) * g


def _sigmoid(x):
    return 1.0 / (1.0 + jnp.exp(-x))


def _silu(x):
    return x * _sigmoid(x)


def _log_sigmoid(x):
    return -(jnp.maximum(-x, 0.0) + jnp.log1p(jnp.exp(-jnp.abs(x))))


def _dot(a, b):
    return jnp.dot(a, b, preferred_element_type=F32)


def _dot_nt(a, b):
    return lax.dot_general(a, b, (((1,), (1,)), ((), ())), preferred_element_type=F32)


def _split3(x):
    hi = x.astype(BF16)
    r = x - hi.astype(F32)
    mid = r.astype(BF16)
    lo = (r - mid.astype(F32)).astype(BF16)
    return hi, mid, lo


def _lane_tile(x, width):
    return jnp.concatenate([x] * (width // LANES), axis=1)


def _swiglu_residual(x, g_ref, wg_ref, wu_ref, wo_ref):
    h = _rms(x, g_ref[...]).astype(BF16)
    acc = None
    for c in range(wo_ref.shape[0] // MXU_WIDTH):
        cols = slice(c * MXU_WIDTH, (c + 1) * MXU_WIDTH)
        gate = _dot(h, wg_ref[:, cols])
        up = _dot(h, wu_ref[:, cols])
        part = _dot((_silu(gate) * up).astype(BF16), wo_ref[cols, :])
        acc = part if acc is None else acc + part
    return x + 0.5 * acc


def _ffn_kernel(x_ref, g_ref, wg_ref, wu_ref, wo_ref, o_ref):
    o_ref[...] = _swiglu_residual(x_ref[...], g_ref, wg_ref, wu_ref, wo_ref)


def _resident(shape, index=None):
    index = index or (0,) * len(shape)
    return pl.BlockSpec(shape, lambda i: index, pipeline_mode=pl.Buffered(1))


def _ffn_specs(d, dff, layer):
    return [_resident((1, d)), _resident((None, d, dff), (layer, 0, 0)),
            _resident((None, d, dff), (layer, 0, 1)), _resident((None, dff, d), (layer, 0, 0))]


def _ffn(x, g, w_in, w_out, layer, tm):
    m, d = x.shape
    dff = w_out.shape[1]
    assert dff % MXU_WIDTH == 0
    return pl.pallas_call(
        _ffn_kernel,
        name="ffn",
        grid=(m // tm,),
        in_specs=[pl.BlockSpec((tm, d), lambda i: (i, 0))] + _ffn_specs(d, dff, layer),
        out_specs=pl.BlockSpec((tm, d), lambda i: (i, 0)),
        out_shape=jax.ShapeDtypeStruct((m, d), F32),
        compiler_params=_params(("arbitrary",)),
    )(x, g.reshape(1, d), w_in, w_in, w_out)


def _lane_cumsum(x):
    n = x.shape[1]
    idx = lax.broadcasted_iota(jnp.int32, x.shape, 1)
    s = 1
    while s < n:
        x = x + jnp.where(idx >= s, pltpu.roll(x, s, axis=1), 0.0)
        s *= 2
    return x


def _mixin_kernel(n_prev, x_ref, g_ref, wt_ref, wn_ref, bf_ref, qg_ref, kgt_ref, gmat_ref, *rest):
    prev = rest[:2] if n_prev else ()
    qb_ref, ktb_ref, vb_ref, kt_ref, vt_ref, lft_ref, ct_ref, u_ref, carry_ref = rest[len(prev):]
    i = pl.program_id(1)
    tm = x_ref.shape[1]
    h = _rms(x_ref[0], g_ref[...]).astype(BF16)

    zt = _dot_nt(wt_ref[...], h)
    k3 = zt[:ATT_W].reshape(N_HEADS, HEAD_DIM, tm)
    ms = jnp.mean(k3 * k3, axis=1, keepdims=True)
    ktn = (k3 * lax.rsqrt(ms + EPS)).reshape(ATT_W, tm) * _lane_tile(kgt_ref[...], tm)
    vt = zt[ATT_W:2 * ATT_W]
    if n_prev:
        kt_ref[0:n_prev, 0] = prev[0][:, 0]
        vt_ref[0:n_prev, 0] = prev[1][:, 0]
    kt_ref[n_prev, 0] = ktn
    vt_ref[n_prev, 0] = vt
    ktb_ref[0] = ktn.astype(BF16)
    vb_ref[0] = vt.T.astype(BF16)

    lft = _log_sigmoid(zt[2 * ATT_W:2 * ATT_W + N_HEADS] + _lane_tile(bf_ref[...], tm))
    lft_ref[0] = lft

    @pl.when(i == 0)
    def _():
        carry_ref[...] = jnp.zeros_like(carry_ref)

    c = _lane_cumsum(lft) + carry_ref[:, :1]
    ct_ref[0] = c
    carry_ref[...] = jnp.broadcast_to(c[:, -1:], carry_ref.shape)

    z = _dot(h, wn_ref[...])
    q = z[:, :ATT_W]
    ms_q = _dot((q * q).astype(BF16), gmat_ref[...])
    qb_ref[0] = (q * lax.rsqrt(ms_q + EPS) * qg_ref[...] * (HEAD_DIM ** -0.5)).astype(BF16)
    cw = (z.shape[1] - ATT_W) // 2
    u_ref[0] = z[:, ATT_W:ATT_W + cw] * _sigmoid(z[:, ATT_W + cw:])


def _mixin(x, g, wt, wn, bfp, qg, kgt, gmat, tm, prev=()):
    b, t, d = x.shape
    cw = (wn.shape[1] - ATT_W) // 2
    n_prev = prev[0].shape[0] if prev else 0
    const = lambda shape: pl.BlockSpec(shape, lambda bi, i: (0,) * len(shape))
    row = lambda w: pl.BlockSpec((1, tm, w), lambda bi, i: (bi, i, 0))
    col = lambda r: pl.BlockSpec((1, r, tm), lambda bi, i: (bi, 0, i))
    layers = lambda n: pl.BlockSpec((n, 1, ATT_W, tm), lambda bi, i: (0, bi, 0, i))
    stacked = jax.ShapeDtypeStruct((n_prev + 1, b, ATT_W, t), F32)
    return pl.pallas_call(
        functools.partial(_mixin_kernel, n_prev),
        name="mixin",
        grid=(b, t // tm),
        in_specs=[row(d), const((1, d)), const(wt.shape), const(wn.shape), const(bfp.shape),
                  const(qg.shape), const(kgt.shape), const(gmat.shape)] + [layers(n_prev)] * len(prev),
        out_specs=[row(ATT_W), col(ATT_W), row(ATT_W), layers(n_prev + 1), layers(n_prev + 1),
                   col(N_HEADS), col(N_HEADS), row(cw)],
        out_shape=[jax.ShapeDtypeStruct((b, t, ATT_W), BF16), jax.ShapeDtypeStruct((b, ATT_W, t), BF16),
                   jax.ShapeDtypeStruct((b, t, ATT_W), BF16), stacked, stacked,
                   jax.ShapeDtypeStruct((b, N_HEADS, t), F32),
                   jax.ShapeDtypeStruct((b, N_HEADS, t), F32), jax.ShapeDtypeStruct((b, t, cw), F32)],
        scratch_shapes=[pltpu.VMEM((N_HEADS, LANES), F32)],
        compiler_params=_params(("arbitrary", "arbitrary")),
    )(x, g.reshape(1, d), wt, wn, bfp, qg, kgt, gmat, *prev)


def _head_pair(q_ref):
    q = q_ref[0]
    first = lax.broadcasted_iota(jnp.int32, q.shape, 1) < HEAD_DIM
    zero = jnp.zeros_like(q)
    return (jnp.where(first, q, zero), jnp.where(first, zero, q)), first


def _attn_fixed_kernel(sb_ref, q_ref, kt_ref, v_ref, c2_ref, c8_ref, o_ref, acc_ref):
    hp = pl.program_id(1)
    i = pl.program_id(2)
    tq = q_ref.shape[1]
    q_heads, first = _head_pair(q_ref)
    q0 = pl.multiple_of(i * tq, tq)
    c_base = c2_ref[0, 0, :, pl.ds(q0, LANES)][:, :1]

    slab = jnp.concatenate([c8_ref[0, :, pl.ds(q0, tq)], jnp.zeros((LANES - N_HEADS, tq), F32)], axis=0)
    by_row = slab.T
    head_lane = lax.broadcasted_iota(jnp.int32, by_row.shape, 1)
    shift = []
    for hh in range(2):
        mine = jnp.sum(jnp.where(head_lane == 2 * hp + hh, by_row, 0.0), axis=1, keepdims=True)
        shift.append(mine - c_base[hh:hh + 1, :] - sb_ref[0])

    c_all = c2_ref[0, 0]
    col = lax.broadcasted_iota(jnp.int32, c_all.shape, 1)
    block_end = jnp.logical_and(col % tq == tq - 1, col < q0)
    alive = (c_base - c_all) + 1e-5 * jnp.abs(c_all) >= -SKIP_THRESH_BF16
    oldest = jnp.where(jnp.logical_and(block_end, alive), col // tq, i)
    reach = [i - jnp.min(oldest[hh:hh + 1, :]) for hh in range(2)]
    shared = jnp.minimum(reach[0], reach[1])

    one = jnp.ones((tq, LANES), BF16)

    def block(j):
        k0 = pl.multiple_of(j * tq, tq)
        kt = kt_ref[0, :, pl.ds(k0, tq)]
        vb = v_ref[0, pl.ds(k0, tq), :]
        ck = c2_ref[0, 0, :, pl.ds(k0, tq)] - c_base
        first_k = lax.broadcasted_iota(jnp.int32, vb.shape, 1) < HEAD_DIM
        v_aug = lambda hh: jnp.where(first_k, vb, one) if hh == 0 else jnp.where(first_k, one, vb)
        return kt, ck, v_aug

    def causal(x):
        r = lax.broadcasted_iota(jnp.int32, x.shape, 0)
        cidx = lax.broadcasted_iota(jnp.int32, x.shape, 1)
        return jnp.where(cidx <= r, x, NEG)

    kt, ck, v_aug = block(i)
    for hh in range(2):
        x = (_dot(q_heads[hh], kt) + shift[hh]) - ck[hh:hh + 1, :]
        acc_ref[hh] = _dot(jnp.exp(causal(x)).astype(BF16), v_aug(hh))

    def step(j, heads):
        kt, ck, v_aug = block(j)
        for hh in heads:
            x = (_dot(q_heads[hh], kt) + shift[hh]) - ck[hh:hh + 1, :]
            acc_ref[hh] += _dot(jnp.exp(x).astype(BF16), v_aug(hh))

    def sweep(start, stop, heads):
        def body(n, carry):
            step(i - 1 - n, heads)
            return carry
        lax.fori_loop(start, stop, body, 0)

    sweep(0, shared, (0, 1))
    sweep(shared, reach[0], (0,))
    sweep(shared, reach[1], (1,))

    a0, a1 = acc_ref[0], acc_ref[1]
    out = jnp.where(first, a0 / pltpu.roll(a0, HEAD_DIM, axis=1), a1 / pltpu.roll(a1, HEAD_DIM, axis=1))
    o_ref[0] = out.astype(o_ref.dtype)


def _attn_online_kernel(sb_ref, q_ref, kt_ref, v_ref, c2_ref, c8_ref, o_ref, m_ref, l_ref, acc_ref):
    del c8_ref
    i = pl.program_id(2)
    tq = q_ref.shape[1]
    q_heads, first = _head_pair(q_ref)
    q0 = pl.multiple_of(i * tq, tq)
    c_base = c2_ref[0, 0, :, pl.ds(q0, LANES)][:, :1]
    s_bound = sb_ref[0]

    m_ref[...] = jnp.full_like(m_ref, NEG)
    l_ref[...] = jnp.zeros_like(l_ref)
    acc_ref[...] = jnp.zeros_like(acc_ref)

    def step(j, diagonal):
        k0 = pl.multiple_of(j * tq, tq)
        kt = kt_ref[0, :, pl.ds(k0, tq)]
        vb = v_ref[0, pl.ds(k0, tq), :]
        ck = c2_ref[0, 0, :, pl.ds(k0, tq)] - c_base
        for hh in range(2):
            s = _dot(q_heads[hh], kt) - ck[hh:hh + 1, :]
            if diagonal:
                r = lax.broadcasted_iota(jnp.int32, s.shape, 0)
                cidx = lax.broadcasted_iota(jnp.int32, s.shape, 1)
                s = jnp.where(cidx <= r, s, NEG)
            m_prev = m_ref[hh]
            m_new = jnp.maximum(m_prev, jnp.max(s, axis=1, keepdims=True))
            p = jnp.exp(s - m_new)
            alpha = jnp.exp(m_prev - m_new)
            l_ref[hh] = alpha * l_ref[hh] + jnp.sum(p, axis=1, keepdims=True)
            acc_ref[hh] = alpha * acc_ref[hh] + _dot(p.astype(BF16), vb)
            m_ref[hh] = m_new

    def needed(j):
        k_last = pl.multiple_of(j * tq, tq) + tq - LANES
        c_last = c2_ref[0, 0, :, pl.ds(k_last, LANES)][:, LANES - 1:]
        best = s_bound - (c_last - c_base) + 1e-5 * jnp.abs(c_last)
        worst = NEG
        for hh in range(2):
            worst = jnp.maximum(worst, jnp.max(best[hh:hh + 1, :] - jnp.min(m_ref[hh])))
        return worst > -SKIP_THRESH

    step(i, True)

    def wanted(j):
        return jnp.logical_and(j >= 0, needed(jnp.maximum(j, 0))).astype(jnp.int32)

    def body(state):
        j, _ = state
        step(j, False)
        return j - 1, wanted(j - 1)

    lax.while_loop(lambda state: state[1] > 0, body, (i - 1, wanted(i - 1)))

    out = jnp.where(first, acc_ref[0] / l_ref[0], acc_ref[1] / l_ref[1])
    o_ref[0] = out.astype(o_ref.dtype)


def _attn_prompt(qb, ktb, vb, ct, s_bound, tq):
    b, t, _ = qb.shape
    pairs = N_HEADS // 2
    c4 = ct.reshape(b, pairs, 2, t)

    def call(body, name, scratch):
        grid_spec = pltpu.PrefetchScalarGridSpec(
            num_scalar_prefetch=1,
            grid=(b, pairs, t // tq),
            in_specs=[
                pl.BlockSpec((1, tq, LANES), lambda bi, hp, i, sb: (bi, i, hp)),
                pl.BlockSpec((1, LANES, t), lambda bi, hp, i, sb: (bi, hp, 0)),
                pl.BlockSpec((1, t, LANES), lambda bi, hp, i, sb: (bi, 0, hp)),
                pl.BlockSpec((1, 1, 2, t), lambda bi, hp, i, sb: (bi, hp, 0, 0)),
                pl.BlockSpec((1, N_HEADS, t), lambda bi, hp, i, sb: (bi, 0, 0)),
            ],
            out_specs=pl.BlockSpec((1, tq, LANES), lambda bi, hp, i, sb: (bi, i, hp)),
            scratch_shapes=scratch,
        )
        return pl.pallas_call(
            body,
            name=name,
            grid_spec=grid_spec,
            out_shape=jax.ShapeDtypeStruct((b, t, ATT_W), BF16),
            compiler_params=_params(("arbitrary", "arbitrary", "arbitrary")),
        )(s_bound, qb, ktb, vb, c4, ct)

    acc = pltpu.VMEM((2, tq, LANES), F32)
    stat = pltpu.VMEM((2, tq, 1), F32)
    return lax.cond(
        2.0 * s_bound[0] <= FIXED_REF_LIMIT,
        lambda: call(_attn_fixed_kernel, "attn_prompt", [acc]),
        lambda: call(_attn_online_kernel, "attn_prompt_online", [stat, stat, acc]),
    )


def _ln_silu(y, g, beta):
    mu = jnp.mean(y, axis=-1, keepdims=True)
    yc = y - mu
    var = jnp.mean(yc * yc, axis=-1, keepdims=True)
    return _silu(yc * lax.rsqrt(var + EPS) * g + beta)


def _conv_kernel(prev_ref, cur_ref, w_ref, b_ref, g_ref, beta_ref, o_ref, ext_ref, y_ref):
    i = pl.program_id(1)
    tm = cur_ref.shape[1]
    halo = prev_ref[0]
    ext_ref[0, 0:CONV_HALO, :] = jnp.where(i == 0, jnp.zeros_like(halo), halo)
    ext_ref[0, CONV_HALO:, :] = cur_ref[0]
    span = tm + CONV_HALO - SUBLANES
    for s in range(1, SUBLANES):
        ext_ref[s, 0:span, :] = ext_ref[0, s:s + span, :]
    first = CONV_HALO - (CONV_W - 1)

    groups = CONV_CHUNK // SUBLANES

    def chunk(ci, carry):
        r0 = pl.multiple_of(ci * CONV_CHUNK, CONV_CHUNK)
        acc = jnp.zeros((CONV_CHUNK, cur_ref.shape[2]), F32) + b_ref[...]
        for j in range(CONV_W):
            whole, s = divmod(first + j, SUBLANES)
            w_rows = jnp.concatenate([w_ref[j]] * groups, axis=0)
            acc = acc + ext_ref[s, pl.ds(r0 + SUBLANES * whole, CONV_CHUNK), :] * w_rows
        y_ref[pl.ds(r0, CONV_CHUNK), :] = acc
        return carry

    lax.fori_loop(0, tm // CONV_CHUNK, chunk, 0)
    o_ref[0] = _ln_silu(y_ref[...], g_ref[...], beta_ref[...]).astype(o_ref.dtype)


def _conv_prompt(u, w, bias, g, beta, tm):
    b, t, c = u.shape
    per = tm // CONV_HALO
    const = lambda shape: pl.BlockSpec(shape, lambda bi, i: (0,) * len(shape))
    w_rows = jnp.broadcast_to(w[:, None, :], (CONV_W, SUBLANES, c))
    return pl.pallas_call(
        _conv_kernel,
        name="conv_prompt",
        grid=(b, t // tm),
        in_specs=[
            pl.BlockSpec((1, CONV_HALO, c), lambda bi, i: (bi, jnp.maximum(i * per - 1, 0), 0)),
            pl.BlockSpec((1, tm, c), lambda bi, i: (bi, i, 0)),
            const((CONV_W, SUBLANES, c)), const((1, c)), const((1, c)), const((1, c)),
        ],
        out_specs=pl.BlockSpec((1, tm, c), lambda bi, i: (bi, i, 0)),
        out_shape=jax.ShapeDtypeStruct((b, t, c), BF16),
        scratch_shapes=[pltpu.VMEM((SUBLANES, tm + CONV_HALO, c), F32), pltpu.VMEM((tm, c), F32)],
        compiler_params=_params(("arbitrary", "arbitrary")),
    )(u, u, w_rows, bias.reshape(1, c), g.reshape(1, c), beta.reshape(1, c))


def _conv_sample_kernel(state_ref, u_ref, w_ref, b_ref, g_ref, beta_ref, o_ref):
    hist = state_ref.shape[0]
    for t in range(u_ref.shape[0]):
        acc = jnp.zeros(u_ref.shape[1:], F32) + b_ref[...]
        for j in range(CONV_W):
            src = t + j
            row = state_ref[src] if src < hist else u_ref[src - hist]
            acc = acc + row * w_ref[j:j + 1, :]
        o_ref[t] = _ln_silu(acc, g_ref[...], beta_ref[...])


def _conv_sample(state, u, w, bias, g, beta):
    s, b, c = u.shape
    return pl.pallas_call(
        _conv_sample_kernel,
        name="conv_sample",
        out_shape=jax.ShapeDtypeStruct((s, b, c), F32),
        compiler_params=pltpu.CompilerParams(vmem_limit_bytes=VMEM_LIMIT),
    )(state, u, w, bias.reshape(1, c), g.reshape(1, c), beta.reshape(1, c))


def _post_kernel(x_ref, a_ref, c_ref, p_ref, wa_ref, wc_ref, g_ref, wg_ref, wu_ref, wo_ref,
                 gp_ref, wpg_ref, wpp_ref, o_ref):
    x = x_ref[...] + _dot(a_ref[...], wa_ref[...]) + _dot(c_ref[...], wc_ref[...])
    x = _swiglu_residual(x, g_ref, wg_ref, wu_ref, wo_ref)
    gate = _sigmoid(_dot(_rms(x, gp_ref[...]).astype(BF16), wpg_ref[...]))
    o_ref[...] = x + gate * _dot(p_ref[...].astype(BF16), wpp_ref[...])


def _post(x, attn, conv, p, lw, tm):
    m, d = x.shape
    dff = lw["ffn2_w_out"].shape[1]
    assert dff % MXU_WIDTH == 0
    row = lambda w: pl.BlockSpec((tm, w), lambda i: (i, 0))
    return pl.pallas_call(
        _post_kernel,
        name="post",
        grid=(m // tm,),
        in_specs=[row(d), row(attn.shape[1]), row(conv.shape[1]), row(p.shape[1]),
                  _resident(lw["wmo_a"].shape), _resident(lw["wmo_c"].shape)]
        + _ffn_specs(d, dff, lw["layer"])
        + [_resident((1, d)), _resident(lw["wg"].shape), _resident(lw["wp"].shape)],
        out_specs=row(d),
        out_shape=jax.ShapeDtypeStruct((m, d), F32),
        compiler_params=_params(("arbitrary",)),
    )(x, attn, conv, p, lw["wmo_a"], lw["wmo_c"], lw["ffn2_norm"].reshape(1, d), lw["ffn2_w_in"],
      lw["ffn2_w_in"], lw["ffn2_w_out"], lw["ple_norm"].reshape(1, d), lw["wg"], lw["wp"])


def _sample_attn_kernel(n_pages, pt_ref, sb_ref, q_ref, kn_ref, vn_ref, lfn_ref, ck_hbm, cv_hbm, clf_hbm,
                        o_ref, kbuf, vbuf, lfbuf, sem, first_slot_ref, qbd_ref, m_ref, l_ref, acc_ref, r_ref):
    bi = pl.program_id(0)
    n_batch = pl.num_programs(0)
    npg = PAGES_PER_GROUP
    n_groups = n_pages // npg
    s_q = q_ref.shape[1]
    rows = s_q * N_HEADS
    page = kbuf.shape[3]
    n_new = kn_ref.shape[2]

    def group_copies(batch, g, slot):
        copies = []
        for pi in range(npg):
            row = pt_ref[batch * n_pages + (n_pages - 1 - (g * npg + pi))]
            copies.append(pltpu.make_async_copy(ck_hbm.at[row], kbuf.at[slot, pi], sem.at[slot, 0]))
            copies.append(pltpu.make_async_copy(cv_hbm.at[row], vbuf.at[slot, pi], sem.at[slot, 1]))
            copies.append(pltpu.make_async_copy(clf_hbm.at[row], lfbuf.at[slot, pi], sem.at[slot, 2]))
        return copies

    @pl.when(bi == 0)
    def _():
        first_slot_ref[0] = 0
        for copy in group_copies(0, 0, 0):
            copy.start()

    first_slot = first_slot_ref[0]

    row_i = lax.broadcasted_iota(jnp.int32, (rows, ATT_W), 0)
    lane_i = lax.broadcasted_iota(jnp.int32, (rows, ATT_W), 1)
    own_head = (lane_i // HEAD_DIM) == (row_i % N_HEADS)
    q = q_ref[0]
    q_rep = jnp.broadcast_to(q[:, None, :], (s_q, N_HEADS, ATT_W)).reshape(rows, ATT_W)
    qbd_ref[...] = jnp.where(own_head, q_rep, 0.0).astype(BF16)
    m_ref[...] = jnp.full_like(m_ref, NEG)
    l_ref[...] = jnp.zeros_like(l_ref)
    acc_ref[...] = jnp.zeros_like(acc_ref)

    jj = lax.broadcasted_iota(jnp.int32, (page, 2 * page), 0)
    ss = lax.broadcasted_iota(jnp.int32, (page, 2 * page), 1)
    later_and_all = jnp.where(jnp.logical_or(jj > ss, ss >= page), 1.0, 0.0).astype(BF16)

    def forget_sums(lft_keys, weights):
        both = sum(_dot(part, weights) for part in _split3(lft_keys))
        half = weights.shape[1] // 2
        return both[:, :half], both[:, half:]

    def logits(kt_keys, bias):
        return _dot(qbd_ref[...], kt_keys.astype(BF16)) + jnp.concatenate([bias] * s_q, axis=0)

    def attend(s, vt_pages):
        m_prev = m_ref[...]
        m_new = jnp.maximum(m_prev, jnp.max(s, axis=1, keepdims=True))
        p = jnp.exp(s - m_new).astype(BF16)
        alpha = jnp.exp(m_prev - m_new)
        l_ref[...] = alpha * l_ref[...] + jnp.sum(p.astype(F32), axis=1, keepdims=True)
        pv, k0 = 0.0, 0
        for vt in vt_pages:
            pv = pv + _dot_nt(p[:, k0:k0 + vt.shape[1]], vt.astype(BF16))
            k0 += vt.shape[1]
        acc_ref[...] = alpha * acc_ref[...] + pv
        m_ref[...] = m_new

    jn = lax.broadcasted_iota(jnp.int32, (n_new, 2 * n_new), 0)
    sn = lax.broadcasted_iota(jnp.int32, (n_new, 2 * n_new), 1)
    same_row_later = jnp.logical_and(jn > sn, jn // s_q == sn // s_q)
    mine = jnp.logical_and(sn >= n_new, jn // s_q == bi)
    suffix, total = forget_sums(lfn_ref[0], jnp.where(jnp.logical_or(same_row_later, mine), 1.0, 0.0).astype(BF16))
    s_new = logits(kn_ref[0], suffix)
    key_i = lax.broadcasted_iota(jnp.int32, s_new.shape, 1)
    q_i = lax.broadcasted_iota(jnp.int32, s_new.shape, 0) // N_HEADS
    admitted = jnp.logical_and(key_i // s_q == bi, key_i % s_q <= q_i)
    attend(jnp.where(admitted, s_new, NEG), [vn_ref[0]])
    r_ref[...] = total

    def body(state):
        g, _ = state
        slot = (first_slot + g) % 2
        for copy in group_copies(bi, g, slot):
            copy.wait()
        run = r_ref[...]
        biases = []
        for pi in range(npg):
            suffix, total = forget_sums(lfbuf[slot, pi], later_and_all)
            biases.append(suffix + run)
            run = run + total
        r_ref[...] = run
        reach = sb_ref[0] + jnp.concatenate([run + 1e-5 * jnp.abs(run)] * s_q, axis=0) - m_ref[...]
        more = jnp.logical_and(g + 1 < n_groups, jnp.max(reach) >= -SKIP_THRESH_BF16)

        @pl.when(more)
        def _():
            for copy in group_copies(bi, g + 1, 1 - slot):
                copy.start()

        @pl.when(jnp.logical_and(jnp.logical_not(more), bi + 1 < n_batch))
        def _():
            for copy in group_copies(bi + 1, 0, 1 - slot):
                copy.start()

        s = jnp.concatenate([logits(kbuf[slot, pi], biases[pi]) for pi in range(npg)], axis=1)
        attend(s, [vbuf[slot, pi] for pi in range(npg)])
        return g + 1, more.astype(jnp.int32)

    done, _ = lax.while_loop(lambda state: state[1] > 0, body, (jnp.int32(0), jnp.int32(1)))
    first_slot_ref[0] = (first_slot + done) % 2

    out = jnp.where(own_head, acc_ref[...] / l_ref[...], 0.0)
    o_ref[0] = jnp.sum(out.reshape(s_q, N_HEADS, ATT_W), axis=1)


def _sample_attn(pt, s_bound, q, kt_new, vt_new, lft_new, cache_kt, cache_vt, cache_lft, n_pages):
    b, s_q, _ = q.shape
    page = cache_kt.shape[2]
    n_new = kt_new.shape[2]
    npg = PAGES_PER_GROUP
    assert n_pages % npg == 0 and n_new == page

    def per_batch(shape):
        return pl.BlockSpec((1,) + shape, lambda bi, pt_ref, sb_ref: (bi, 0, 0))

    def shared(shape):
        return pl.BlockSpec((1,) + shape, lambda bi, pt_ref, sb_ref: (0, 0, 0))

    in_hbm = pl.BlockSpec(memory_space=pl.ANY)
    rows = s_q * N_HEADS
    grid_spec = pltpu.PrefetchScalarGridSpec(
        num_scalar_prefetch=2,
        grid=(b,),
        in_specs=[per_batch((s_q, ATT_W)), shared((ATT_W, n_new)), shared((ATT_W, n_new)),
                  shared((N_HEADS, n_new)), in_hbm, in_hbm, in_hbm],
        out_specs=per_batch((s_q, ATT_W)),
        scratch_shapes=[pltpu.VMEM((2, npg, ATT_W, page), F32), pltpu.VMEM((2, npg, ATT_W, page), F32),
                        pltpu.VMEM((2, npg, N_HEADS, page), F32), pltpu.SemaphoreType.DMA((2, 3)),
                        pltpu.SMEM((1,), jnp.int32),
                        pltpu.VMEM((rows, ATT_W), BF16), pltpu.VMEM((rows, 1), F32),
                        pltpu.VMEM((rows, 1), F32), pltpu.VMEM((rows, ATT_W), F32),
                        pltpu.VMEM((N_HEADS, page), F32)],
    )
    return pl.pallas_call(
        functools.partial(_sample_attn_kernel, n_pages),
        name="attn_sample",
        grid_spec=grid_spec,
        out_shape=jax.ShapeDtypeStruct((b, s_q, ATT_W), F32),
        compiler_params=_params(("arbitrary",)),
    )(pt, s_bound, q, kt_new, vt_new, lft_new, cache_kt, cache_vt, cache_lft)


def _layer_weights(l, w):
    wmi = w["w_mix_in"][l]
    d = wmi.shape[0]
    n_qkv = 3 * ATT_W
    pad_rows = BF16_ROWS - N_HEADS
    wt = jnp.concatenate([wmi[:, ATT_W:n_qkv + N_HEADS].T, jnp.zeros((pad_rows, d), F32)], axis=0)
    wn = jnp.concatenate([wmi[:, :ATT_W], wmi[:, n_qkv + N_HEADS:]], axis=1)
    bfp = jnp.broadcast_to(w["b_forget"][l][:, None], (N_HEADS, LANES))
    kgt = jnp.broadcast_to(jnp.tile(w["k_norm"][l], N_HEADS)[:, None], (ATT_W, LANES))
    head_of = jnp.arange(ATT_W) // HEAD_DIM
    gmat = jnp.where(head_of[:, None] == head_of[None, :], 1.0 / HEAD_DIM, 0.0).astype(BF16)
    s_bound = (HEAD_DIM ** 0.5) * jnp.max(jnp.abs(w["q_norm"][l])) * jnp.max(jnp.abs(w["k_norm"][l])) * 1.02
    wmo = w["w_mix_out"][l].astype(BF16)
    return dict(
        layer=l, ffn1_norm=w["ffn1_norm"][l], ffn1_w_in=w["ffn1_w_in_bf"], ffn1_w_out=w["ffn1_w_out_bf"],
        mix_norm=w["mix_norm"][l], wt=wt.astype(BF16), wn=wn.astype(BF16), bfp=bfp,
        qg=jnp.tile(w["q_norm"][l], N_HEADS).reshape(1, ATT_W), kgt=kgt, gmat=gmat,
        s_bound=s_bound.reshape(1).astype(F32),
        conv_w=w["conv_w"][l], conv_b=w["conv_b"][l], conv_ln_g=w["conv_ln_g"][l],
        conv_ln_b=w["conv_ln_b"][l], wmo_a=wmo[:ATT_W], wmo_c=wmo[ATT_W:],
        ffn2_norm=w["ffn2_norm"][l], ffn2_w_in=w["ffn2_w_in_bf"], ffn2_w_out=w["ffn2_w_out_bf"],
        ple_norm=w["ple_norm"][l], wg=w["w_ple_gate"][l].astype(BF16), wp=w["w_ple_proj"][l].astype(BF16),
    )


def _mixin_call(x3, lw, tm, prev=()):
    return _mixin(x3, lw["mix_norm"], lw["wt"], lw["wn"], lw["bfp"], lw["qg"], lw["kgt"], lw["gmat"], tm, prev)


def kernel(x_prompt, x_sample, p_prompt, p_sample, cache_k, cache_v, cache_logf, state_conv, page_table, ffn1_norm, ffn1_w_in, ffn1_w_out, mix_norm, w_mix_in, b_forget, q_norm, k_norm, conv_w, conv_b, conv_ln_g, conv_ln_b, w_mix_out, ffn2_norm, ffn2_w_in, ffn2_w_out, ple_norm, w_ple_gate, w_ple_proj):
    depth = ffn1_norm.shape[0]
    b, t, d = x_prompt.shape
    sb, ss, _ = x_sample.shape
    n_pool, page = cache_k.shape[1], cache_k.shape[2]
    n_pages = page_table.shape[1]
    hist = state_conv.shape[2]
    ms = sb * ss

    w = dict(ffn1_norm=ffn1_norm, mix_norm=mix_norm, w_mix_in=w_mix_in, b_forget=b_forget, q_norm=q_norm,
             k_norm=k_norm, conv_w=conv_w, conv_b=conv_b, conv_ln_g=conv_ln_g, conv_ln_b=conv_ln_b,
             w_mix_out=w_mix_out, ffn2_norm=ffn2_norm, ple_norm=ple_norm, w_ple_gate=w_ple_gate,
             w_ple_proj=w_ple_proj,
             ffn1_w_in_bf=ffn1_w_in.astype(BF16), ffn1_w_out_bf=ffn1_w_out.astype(BF16),
             ffn2_w_in_bf=ffn2_w_in.astype(BF16), ffn2_w_out_bf=ffn2_w_out.astype(BF16))

    ckt = jnp.transpose(cache_k, (0, 1, 3, 4, 2)).reshape(depth * n_pool, ATT_W, page)
    cvt = jnp.transpose(cache_v, (0, 1, 3, 4, 2)).reshape(depth * n_pool, ATT_W, page)
    clft = jnp.transpose(cache_logf, (0, 1, 3, 2)).reshape(depth * n_pool, N_HEADS, page)
    state_t = jnp.transpose(state_conv, (0, 2, 1, 3))

    xp = x_prompt.reshape(b * t, d)
    xs = x_sample.reshape(ms, d)
    outs = {name: [] for name in ("fp", "cp", "ks", "vs", "fs", "cs")}
    kv_prompt = ()

    for l in range(depth):
        lw = _layer_weights(l, w)

        xp = _ffn(xp, lw["ffn1_norm"], lw["ffn1_w_in"], lw["ffn1_w_out"], l, FFN_ROW_TILE)
        qb, ktb, vb, kt, vt, lft, ct, u = _mixin_call(xp.reshape(b, t, d), lw, ROW_TILE, kv_prompt)
        kv_prompt = (kt, vt)
        attn = _attn_prompt(qb, ktb, vb, ct, lw["s_bound"], ATT_TILE)
        conv = _conv_prompt(u, lw["conv_w"], lw["conv_b"], lw["conv_ln_g"], lw["conv_ln_b"], ROW_TILE)
        xp = _post(xp, attn.reshape(b * t, ATT_W), conv.reshape(b * t, -1),
                   p_prompt[l].reshape(b * t, -1), lw, ROW_TILE)
        outs["fp"].append(jnp.transpose(lft, (0, 2, 1)))
        outs["cp"].append(u[:, t - hist:, :])

        xs = _ffn(xs, lw["ffn1_norm"], lw["ffn1_w_in"], lw["ffn1_w_out"], l, ms)
        qb, _, _, kt, vt, lft, _, u = _mixin_call(xs.reshape(1, ms, d), lw, ms)
        kt, vt = kt[0], vt[0]
        pt = (page_table + l * n_pool).reshape(-1)
        attn = _sample_attn(pt, lw["s_bound"], qb.reshape(sb, ss, ATT_W).astype(F32), kt, vt, lft,
                            ckt, cvt, clft, n_pages)
        u_t = jnp.transpose(u.reshape(sb, ss, -1), (1, 0, 2))
        conv = _conv_sample(state_t[l], u_t, lw["conv_w"], lw["conv_b"], lw["conv_ln_g"], lw["conv_ln_b"])
        conv = jnp.transpose(conv, (1, 0, 2)).reshape(ms, -1)
        xs = _post(xs, attn.reshape(ms, ATT_W).astype(BF16), conv.astype(BF16),
                   p_sample[l].reshape(ms, -1), lw, ms)
        outs["ks"].append(kt[0].T.reshape(sb, ss, N_HEADS, HEAD_DIM))
        outs["vs"].append(vt[0].T.reshape(sb, ss, N_HEADS, HEAD_DIM))
        outs["fs"].append(lft[0].T.reshape(sb, ss, N_HEADS))
        outs["cs"].append(jnp.transpose(jnp.concatenate([state_t[l], u_t], axis=0)[ss:], (1, 0, 2)))

    stack = lambda name: jnp.stack(outs[name])
    heads_last = lambda a: jnp.transpose(a.reshape(depth, b, N_HEADS, HEAD_DIM, t), (0, 1, 4, 2, 3))
    return (xp.reshape(b, t, d), xs.reshape(sb, ss, d), heads_last(kv_prompt[0]), heads_last(kv_prompt[1]),
            stack("fp"), stack("cp"), stack("ks"), stack("vs"), stack("fs"), stack("cs"))
```

```python
import functools

import jax
import jax.numpy as jnp
from jax import lax
from jax.experimental import pallas as pl
from jax.experimental.pallas import tpu as pltpu

F32 = jnp.float32
BF16 = jnp.bfloat16

EPS = 1e-6
N_HEADS = 8
HEAD_DIM = 64
ATT_W = N_HEADS * HEAD_DIM
CONV_W = 31
LANES = 128
SUBLANES = 8
BF16_ROWS = 16
MXU_WIDTH = 256
V7X_VMEM_BYTES = 64 * 2 ** 20
VMEM_LIMIT = V7X_VMEM_BYTES - 8 * 2 ** 20
NEG = -1e30
SKIP_THRESH = 106.0
SKIP_THRESH_BF16 = 94.0
FIXED_REF_LIMIT = 60.0

ROW_TILE = 512
ATT_TILE = 512
CONV_HALO = 32
CONV_CHUNK = 32
PAGES_PER_GROUP = 8
FFN_ROW_TILE = 1024


def _params(sem):
    return pltpu.CompilerParams(dimension_semantics=sem, vmem_limit_bytes=VMEM_LIMIT)


def _rms(x, g):
    return x * lax.rsqrt(jnp.mean(x * x, axis=-1, keepdims=True) + EPS) * g


def _sigmoid(x):
    return 1.0 / (1.0 + jnp.exp(-x))


def _silu(x):
    return x * _sigmoid(x)


def _log_sigmoid(x):
    return -(jnp.maximum(-x, 0.0) + jnp.log1p(jnp.exp(-jnp.abs(x))))


def _dot(a, b):
    return jnp.dot(a, b, preferred_element_type=F32)


def _dot_nt(a, b):
    return lax.dot_general(a, b, (((1,), (1,)), ((), ())), preferred_element_type=F32)


def _split3(x):
    hi = x.astype(BF16)
    r = x - hi.astype(F32)
    mid = r.astype(BF16)
    lo = (r - mid.astype(F32)).astype(BF16)
    return hi, mid, lo


def _lane_tile(x, width):
    return jnp.concatenate([x] * (width // LANES), axis=1)


def _swiglu_residual(x, g_ref, wg_ref, wu_ref, wo_ref):
    h = _rms(x, g_ref[...]).astype(BF16)
    acc = None
    for c in range(wo_ref.shape[0] // MXU_WIDTH):
        cols = slice(c * MXU_WIDTH, (c + 1) * MXU_WIDTH)
        gate = _dot(h, wg_ref[:, cols])
        up = _dot(h, wu_ref[:, cols])
        part = _dot((_silu(gate) * up).astype(BF16), wo_ref[cols, :])
        acc = part if acc is None else acc + part
    return x + 0.5 * acc


def _ffn_kernel(x_ref, g_ref, wg_ref, wu_ref, wo_ref, o_ref):
    o_ref[...] = _swiglu_residual(x_ref[...], g_ref, wg_ref, wu_ref, wo_ref)


def _resident(shape, index=None):
    index = index or (0,) * len(shape)
    return pl.BlockSpec(shape, lambda i: index, pipeline_mode=pl.Buffered(1))


def _ffn_specs(d, dff, layer):
    return [_resident((1, d)), _resident((None, d, dff), (layer, 0, 0)),
            _resident((None, d, dff), (layer, 0, 1)), _resident((None, dff, d), (layer, 0, 0))]


def _ffn(x, g, w_in, w_out, layer, tm):
    m, d = x.shape
    dff = w_out.shape[1]
    assert dff % MXU_WIDTH == 0
    return pl.pallas_call(
        _ffn_kernel,
        name="ffn",
        grid=(m // tm,),
        in_specs=[pl.BlockSpec((tm, d), lambda i: (i, 0))] + _ffn_specs(d, dff, layer),
        out_specs=pl.BlockSpec((tm, d), lambda i: (i, 0)),
        out_shape=jax.ShapeDtypeStruct((m, d), F32),
        compiler_params=_params(("arbitrary",)),
    )(x, g.reshape(1, d), w_in, w_in, w_out)


def _lane_cumsum(x):
    n = x.shape[1]
    idx = lax.broadcasted_iota(jnp.int32, x.shape, 1)
    s = 1
    while s < n:
        x = x + jnp.where(idx >= s, pltpu.roll(x, s, axis=1), 0.0)
        s *= 2
    return x


def _mixin_kernel(n_prev, x_ref, g_ref, wt_ref, wn_ref, bf_ref, qg_ref, kgt_ref, gmat_ref, *rest):
    prev = rest[:2] if n_prev else ()
    qb_ref, ktb_ref, vb_ref, kt_ref, vt_ref, lft_ref, ct_ref, cs_ref, u_ref, carry_ref = rest[len(prev):]
    i = pl.program_id(1)
    tm = x_ref.shape[1]
    h = _rms(x_ref[0], g_ref[...]).astype(BF16)

    zt = _dot_nt(wt_ref[...], h)
    k3 = zt[:ATT_W].reshape(N_HEADS, HEAD_DIM, tm)
    ms = jnp.mean(k3 * k3, axis=1, keepdims=True)
    ktn = (k3 * lax.rsqrt(ms + EPS)).reshape(ATT_W, tm) * _lane_tile(kgt_ref[...], tm)
    vt = zt[ATT_W:2 * ATT_W]
    if n_prev:
        kt_ref[0:n_prev, 0] = prev[0][:, 0]
        vt_ref[0:n_prev, 0] = prev[1][:, 0]
    kt_ref[n_prev, 0] = ktn
    vt_ref[n_prev, 0] = vt
    ktb_ref[0] = ktn.astype(BF16)
    vb_ref[0] = vt.T.astype(BF16)

    lft = _log_sigmoid(zt[2 * ATT_W:2 * ATT_W + N_HEADS] + _lane_tile(bf_ref[...], tm))
    lft_ref[0] = lft

    @pl.when(i == 0)
    def _():
        carry_ref[...] = jnp.zeros_like(carry_ref)

    c = _lane_cumsum(lft) + carry_ref[:, :1]
    ct_ref[0] = c
    carry_ref[...] = jnp.broadcast_to(c[:, -1:], carry_ref.shape)
    by_row = jnp.concatenate([c, jnp.zeros((LANES - N_HEADS, tm), F32)], axis=0).T
    for hp in range(N_HEADS // 2):
        cs_ref[0, hp] = by_row if hp == 0 else pltpu.roll(by_row, LANES - 2 * hp, axis=1)

    z = _dot(h, wn_ref[...])
    q = z[:, :ATT_W]
    ms_q = _dot((q * q).astype(BF16), gmat_ref[...])
    qb_ref[0] = (q * lax.rsqrt(ms_q + EPS) * qg_ref[...] * (HEAD_DIM ** -0.5)).astype(BF16)
    cw = (z.shape[1] - ATT_W) // 2
    u_ref[0] = z[:, ATT_W:ATT_W + cw] * _sigmoid(z[:, ATT_W + cw:])


def _mixin(x, g, wt, wn, bfp, qg, kgt, gmat, tm, prev=()):
    b, t, d = x.shape
    cw = (wn.shape[1] - ATT_W) // 2
    n_prev = prev[0].shape[0] if prev else 0
    const = lambda shape: pl.BlockSpec(shape, lambda bi, i: (0,) * len(shape))
    row = lambda w: pl.BlockSpec((1, tm, w), lambda bi, i: (bi, i, 0))
    col = lambda r: pl.BlockSpec((1, r, tm), lambda bi, i: (bi, 0, i))
    layers = lambda n: pl.BlockSpec((n, 1, ATT_W, tm), lambda bi, i: (0, bi, 0, i))
    stacked = jax.ShapeDtypeStruct((n_prev + 1, b, ATT_W, t), F32)
    return pl.pallas_call(
        functools.partial(_mixin_kernel, n_prev),
        name="mixin",
        grid=(b, t // tm),
        in_specs=[row(d), const((1, d)), const(wt.shape), const(wn.shape), const(bfp.shape),
                  const(qg.shape), const(kgt.shape), const(gmat.shape)] + [layers(n_prev)] * len(prev),
        out_specs=[row(ATT_W), col(ATT_W), row(ATT_W), layers(n_prev + 1), layers(n_prev + 1),
                   col(N_HEADS), col(N_HEADS),
                   pl.BlockSpec((1, N_HEADS // 2, tm, LANES), lambda bi, i: (bi, 0, i, 0)), row(cw)],
        out_shape=[jax.ShapeDtypeStruct((b, t, ATT_W), BF16), jax.ShapeDtypeStruct((b, ATT_W, t), BF16),
                   jax.ShapeDtypeStruct((b, t, ATT_W), BF16), stacked, stacked,
                   jax.ShapeDtypeStruct((b, N_HEADS, t), F32), jax.ShapeDtypeStruct((b, N_HEADS, t), F32),
                   jax.ShapeDtypeStruct((b, N_HEADS // 2, t, LANES), F32),
                   jax.ShapeDtypeStruct((b, t, cw), F32)],
        scratch_shapes=[pltpu.VMEM((N_HEADS, LANES), F32)],
        compiler_params=_params(("arbitrary", "arbitrary")),
    )(x, g.reshape(1, d), wt, wn, bfp, qg, kgt, gmat, *prev)


def _head_pair(q_ref):
    q = q_ref[0]
    first = lax.broadcasted_iota(jnp.int32, q.shape, 1) < HEAD_DIM
    zero = jnp.zeros_like(q)
    return (jnp.where(first, q, zero), jnp.where(first, zero, q)), first


def _attn_fixed_kernel(sb_ref, q_ref, kt_ref, v_ref, c2_ref, cs_ref, o_ref, acc_ref):
    i = pl.program_id(2)
    tq = q_ref.shape[1]
    q_heads, first = _head_pair(q_ref)
    q0 = pl.multiple_of(i * tq, tq)
    c_base = c2_ref[0, 0, :, pl.ds(q0, LANES)][:, :1]

    by_row = cs_ref[0, 0]
    shift = [by_row[:, hh:hh + 1] - c_base[hh:hh + 1, :] - sb_ref[0] for hh in range(2)]

    c_all = c2_ref[0, 0]
    col = lax.broadcasted_iota(jnp.int32, c_all.shape, 1)
    block_end = jnp.logical_and(col % tq == tq - 1, col < q0)
    alive = (c_base - c_all) + 1e-5 * jnp.abs(c_all) >= -SKIP_THRESH_BF16
    oldest = jnp.where(jnp.logical_and(block_end, alive), col // tq, i)
    reach = [i - jnp.min(oldest[hh:hh + 1, :]) for hh in range(2)]
    shared = jnp.minimum(reach[0], reach[1])

    one = jnp.ones((tq, LANES), BF16)

    def block(j):
        k0 = pl.multiple_of(j * tq, tq)
        kt = kt_ref[0, :, pl.ds(k0, tq)]
        vb = v_ref[0, pl.ds(k0, tq), :]
        ck = c2_ref[0, 0, :, pl.ds(k0, tq)] - c_base
        first_k = lax.broadcasted_iota(jnp.int32, vb.shape, 1) < HEAD_DIM
        v_aug = lambda hh: jnp.where(first_k, vb, one) if hh == 0 else jnp.where(first_k, one, vb)
        return kt, ck, v_aug

    def causal(x):
        r = lax.broadcasted_iota(jnp.int32, x.shape, 0)
        cidx = lax.broadcasted_iota(jnp.int32, x.shape, 1)
        return jnp.where(cidx <= r, x, NEG)

    kt, ck, v_aug = block(i)
    for hh in range(2):
        x = (_dot(q_heads[hh], kt) + shift[hh]) - ck[hh:hh + 1, :]
        acc_ref[hh] = _dot(jnp.exp(causal(x)).astype(BF16), v_aug(hh))

    def step(j, heads):
        kt, ck, v_aug = block(j)
        for hh in heads:
            x = (_dot(q_heads[hh], kt) + shift[hh]) - ck[hh:hh + 1, :]
            acc_ref[hh] += _dot(jnp.exp(x).astype(BF16), v_aug(hh))

    def sweep(start, stop, heads):
        def body(n, carry):
            step(i - 1 - n, heads)
            return carry
        lax.fori_loop(start, stop, body, 0)

    sweep(0, shared, (0, 1))
    sweep(shared, reach[0], (0,))
    sweep(shared, reach[1], (1,))

    a0, a1 = acc_ref[0], acc_ref[1]
    out = jnp.where(first, a0 / pltpu.roll(a0, HEAD_DIM, axis=1), a1 / pltpu.roll(a1, HEAD_DIM, axis=1))
    o_ref[0] = out.astype(o_ref.dtype)


def _attn_online_kernel(sb_ref, q_ref, kt_ref, v_ref, c2_ref, cs_ref, o_ref, m_ref, l_ref, acc_ref):
    del cs_ref
    i = pl.program_id(2)
    tq = q_ref.shape[1]
    q_heads, first = _head_pair(q_ref)
    q0 = pl.multiple_of(i * tq, tq)
    c_base = c2_ref[0, 0, :, pl.ds(q0, LANES)][:, :1]
    s_bound = sb_ref[0]

    m_ref[...] = jnp.full_like(m_ref, NEG)
    l_ref[...] = jnp.zeros_like(l_ref)
    acc_ref[...] = jnp.zeros_like(acc_ref)

    def step(j, diagonal):
        k0 = pl.multiple_of(j * tq, tq)
        kt = kt_ref[0, :, pl.ds(k0, tq)]
        vb = v_ref[0, pl.ds(k0, tq), :]
        ck = c2_ref[0, 0, :, pl.ds(k0, tq)] - c_base
        for hh in range(2):
            s = _dot(q_heads[hh], kt) - ck[hh:hh + 1, :]
            if diagonal:
                r = lax.broadcasted_iota(jnp.int32, s.shape, 0)
                cidx = lax.broadcasted_iota(jnp.int32, s.shape, 1)
                s = jnp.where(cidx <= r, s, NEG)
            m_prev = m_ref[hh]
            m_new = jnp.maximum(m_prev, jnp.max(s, axis=1, keepdims=True))
            p = jnp.exp(s - m_new)
            alpha = jnp.exp(m_prev - m_new)
            l_ref[hh] = alpha * l_ref[hh] + jnp.sum(p, axis=1, keepdims=True)
            acc_ref[hh] = alpha * acc_ref[hh] + _dot(p.astype(BF16), vb)
            m_ref[hh] = m_new

    def needed(j):
        k_last = pl.multiple_of(j * tq, tq) + tq - LANES
        c_last = c2_ref[0, 0, :, pl.ds(k_last, LANES)][:, LANES - 1:]
        best = s_bound - (c_last - c_base) + 1e-5 * jnp.abs(c_last)
        worst = NEG
        for hh in range(2):
            worst = jnp.maximum(worst, jnp.max(best[hh:hh + 1, :] - jnp.min(m_ref[hh])))
        return worst > -SKIP_THRESH

    step(i, True)

    def wanted(j):
        return jnp.logical_and(j >= 0, needed(jnp.maximum(j, 0))).astype(jnp.int32)

    def body(state):
        j, _ = state
        step(j, False)
        return j - 1, wanted(j - 1)

    lax.while_loop(lambda state: state[1] > 0, body, (i - 1, wanted(i - 1)))

    out = jnp.where(first, acc_ref[0] / l_ref[0], acc_ref[1] / l_ref[1])
    o_ref[0] = out.astype(o_ref.dtype)


def _attn_prompt(qb, ktb, vb, ct, cs, s_bound, tq):
    b, t, _ = qb.shape
    pairs = N_HEADS // 2
    c4 = ct.reshape(b, pairs, 2, t)

    def call(body, name, scratch):
        grid_spec = pltpu.PrefetchScalarGridSpec(
            num_scalar_prefetch=1,
            grid=(b, pairs, t // tq),
            in_specs=[
                pl.BlockSpec((1, tq, LANES), lambda bi, hp, i, sb: (bi, i, hp)),
                pl.BlockSpec((1, LANES, t), lambda bi, hp, i, sb: (bi, hp, 0)),
                pl.BlockSpec((1, t, LANES), lambda bi, hp, i, sb: (bi, 0, hp)),
                pl.BlockSpec((1, 1, 2, t), lambda bi, hp, i, sb: (bi, hp, 0, 0)),
                pl.BlockSpec((1, 1, tq, LANES), lambda bi, hp, i, sb: (bi, hp, i, 0)),
            ],
            out_specs=pl.BlockSpec((1, tq, LANES), lambda bi, hp, i, sb: (bi, i, hp)),
            scratch_shapes=scratch,
        )
        return pl.pallas_call(
            body,
            name=name,
            grid_spec=grid_spec,
            out_shape=jax.ShapeDtypeStruct((b, t, ATT_W), BF16),
            compiler_params=_params(("arbitrary", "arbitrary", "arbitrary")),
        )(s_bound, qb, ktb, vb, c4, cs)

    acc = pltpu.VMEM((2, tq, LANES), F32)
    stat = pltpu.VMEM((2, tq, 1), F32)
    return lax.cond(
        2.0 * s_bound[0] <= FIXED_REF_LIMIT,
        lambda: call(_attn_fixed_kernel, "attn_prompt", [acc]),
        lambda: call(_attn_online_kernel, "attn_prompt_online", [stat, stat, acc]),
    )


def _ln_silu(y, g, beta):
    mu = jnp.mean(y, axis=-1, keepdims=True)
    yc = y - mu
    var = jnp.mean(yc * yc, axis=-1, keepdims=True)
    return _silu(yc * lax.rsqrt(var + EPS) * g + beta)


def _conv_kernel(prev_ref, cur_ref, w_ref, b_ref, o_ref, ext_ref):
    i = pl.program_id(1)
    tm = cur_ref.shape[1]
    halo = prev_ref[0]
    ext_ref[0, 0:CONV_HALO, :] = jnp.where(i == 0, jnp.zeros_like(halo), halo)
    ext_ref[0, CONV_HALO:, :] = cur_ref[0]
    span = tm + CONV_HALO - SUBLANES
    for s in range(1, SUBLANES):
        ext_ref[s, 0:span, :] = ext_ref[0, s:s + span, :]
    first = CONV_HALO - (CONV_W - 1)

    groups = CONV_CHUNK // SUBLANES

    def chunk(ci, carry):
        r0 = pl.multiple_of(ci * CONV_CHUNK, CONV_CHUNK)
        acc = jnp.zeros((CONV_CHUNK, cur_ref.shape[2]), F32) + b_ref[...]
        for j in range(CONV_W):
            whole, s = divmod(first + j, SUBLANES)
            w_rows = jnp.concatenate([w_ref[j]] * groups, axis=0)
            acc = acc + ext_ref[s, pl.ds(r0 + SUBLANES * whole, CONV_CHUNK), :] * w_rows
        o_ref[0, pl.ds(r0, CONV_CHUNK), :] = acc
        return carry

    lax.fori_loop(0, tm // CONV_CHUNK, chunk, 0)


def _conv_prompt(u, w, bias, tm):
    b, t, c = u.shape
    per = tm // CONV_HALO
    const = lambda shape: pl.BlockSpec(shape, lambda bi, i: (0,) * len(shape))
    w_rows = jnp.broadcast_to(w[:, None, :], (CONV_W, SUBLANES, c))
    return pl.pallas_call(
        _conv_kernel,
        name="conv_prompt",
        grid=(b, t // tm),
        in_specs=[
            pl.BlockSpec((1, CONV_HALO, c), lambda bi, i: (bi, jnp.maximum(i * per - 1, 0), 0)),
            pl.BlockSpec((1, tm, c), lambda bi, i: (bi, i, 0)),
            const((CONV_W, SUBLANES, c)), const((1, c)),
        ],
        out_specs=pl.BlockSpec((1, tm, c), lambda bi, i: (bi, i, 0)),
        out_shape=jax.ShapeDtypeStruct((b, t, c), F32),
        scratch_shapes=[pltpu.VMEM((SUBLANES, tm + CONV_HALO, c), F32)],
        compiler_params=_params(("arbitrary", "arbitrary")),
    )(u, u, w_rows, bias.reshape(1, c))


def _conv_sample_kernel(state_ref, u_ref, w_ref, b_ref, o_ref):
    hist = state_ref.shape[0]
    for t in range(u_ref.shape[0]):
        acc = jnp.zeros(u_ref.shape[1:], F32) + b_ref[...]
        for j in range(CONV_W):
            src = t + j
            row = state_ref[src] if src < hist else u_ref[src - hist]
            acc = acc + row * w_ref[j:j + 1, :]
        o_ref[t] = acc


def _conv_sample(state, u, w, bias):
    s, b, c = u.shape
    return pl.pallas_call(
        _conv_sample_kernel,
        name="conv_sample",
        out_shape=jax.ShapeDtypeStruct((s, b, c), F32),
        compiler_params=pltpu.CompilerParams(vmem_limit_bytes=VMEM_LIMIT),
    )(state, u, w, bias.reshape(1, c))


def _post_kernel(x_ref, a_ref, c_ref, p_ref, lg_ref, lb_ref, wa_ref, wc_ref, g_ref, wg_ref, wu_ref, wo_ref,
                 gp_ref, wpg_ref, wpp_ref, o_ref):
    conv = _ln_silu(c_ref[...], lg_ref[...], lb_ref[...]).astype(BF16)
    x = x_ref[...] + _dot(a_ref[...], wa_ref[...]) + _dot(conv, wc_ref[...])
    x = _swiglu_residual(x, g_ref, wg_ref, wu_ref, wo_ref)
    gate = _sigmoid(_dot(_rms(x, gp_ref[...]).astype(BF16), wpg_ref[...]))
    o_ref[...] = x + gate * _dot(p_ref[...].astype(BF16), wpp_ref[...])


def _post(x, attn, conv, p, lw, tm):
    m, d = x.shape
    c = conv.shape[1]
    dff = lw["ffn2_w_out"].shape[1]
    assert dff % MXU_WIDTH == 0
    row = lambda w: pl.BlockSpec((tm, w), lambda i: (i, 0))
    return pl.pallas_call(
        _post_kernel,
        name="post",
        grid=(m // tm,),
        in_specs=[row(d), row(attn.shape[1]), row(c), row(p.shape[1]), _resident((1, c)), _resident((1, c)),
                  _resident(lw["wmo_a"].shape), _resident(lw["wmo_c"].shape)]
        + _ffn_specs(d, dff, lw["layer"])
        + [_resident((1, d)), _resident(lw["wg"].shape), _resident(lw["wp"].shape)],
        out_specs=row(d),
        out_shape=jax.ShapeDtypeStruct((m, d), F32),
        compiler_params=_params(("arbitrary",)),
    )(x, attn, conv, p, lw["conv_ln_g"].reshape(1, c), lw["conv_ln_b"].reshape(1, c), lw["wmo_a"],
      lw["wmo_c"], lw["ffn2_norm"].reshape(1, d), lw["ffn2_w_in"], lw["ffn2_w_in"], lw["ffn2_w_out"],
      lw["ple_norm"].reshape(1, d), lw["wg"], lw["wp"])


def _sample_attn_kernel(n_pages, pt_ref, sb_ref, q_ref, kn_ref, vn_ref, lfn_ref, ck_hbm, cv_hbm, clf_hbm,
                        o_ref, kbuf, vbuf, lfbuf, sem, first_slot_ref, qbd_ref, m_ref, l_ref, acc_ref, r_ref):
    bi = pl.program_id(0)
    n_batch = pl.num_programs(0)
    npg = PAGES_PER_GROUP
    n_groups = n_pages // npg
    s_q = q_ref.shape[1]
    rows = s_q * N_HEADS
    page = kbuf.shape[3]
    n_new = kn_ref.shape[2]

    def group_copies(batch, g, slot):
        copies = []
        for pi in range(npg):
            row = pt_ref[batch * n_pages + (n_pages - 1 - (g * npg + pi))]
            copies.append(pltpu.make_async_copy(ck_hbm.at[row], kbuf.at[slot, pi], sem.at[slot, 0]))
            copies.append(pltpu.make_async_copy(cv_hbm.at[row], vbuf.at[slot, pi], sem.at[slot, 1]))
            copies.append(pltpu.make_async_copy(clf_hbm.at[row], lfbuf.at[slot, pi], sem.at[slot, 2]))
        return copies

    @pl.when(bi == 0)
    def _():
        first_slot_ref[0] = 0
        for copy in group_copies(0, 0, 0):
            copy.start()

    first_slot = first_slot_ref[0]

    row_i = lax.broadcasted_iota(jnp.int32, (rows, ATT_W), 0)
    lane_i = lax.broadcasted_iota(jnp.int32, (rows, ATT_W), 1)
    own_head = (lane_i // HEAD_DIM) == (row_i % N_HEADS)
    q = q_ref[0]
    q_rep = jnp.broadcast_to(q[:, None, :], (s_q, N_HEADS, ATT_W)).reshape(rows, ATT_W)
    qbd_ref[...] = jnp.where(own_head, q_rep, 0.0).astype(BF16)
    m_ref[...] = jnp.full_like(m_ref, NEG)
    l_ref[...] = jnp.zeros_like(l_ref)
    acc_ref[...] = jnp.zeros_like(acc_ref)

    jj = lax.broadcasted_iota(jnp.int32, (page, 2 * page), 0)
    ss = lax.broadcasted_iota(jnp.int32, (page, 2 * page), 1)
    later_and_all = jnp.where(jnp.logical_or(jj > ss, ss >= page), 1.0, 0.0).astype(BF16)

    def forget_sums(lft_keys, weights):
        both = sum(_dot(part, weights) for part in _split3(lft_keys))
        half = weights.shape[1] // 2
        return both[:, :half], both[:, half:]

    def logits(kt_keys, bias):
        return _dot(qbd_ref[...], kt_keys.astype(BF16)) + jnp.concatenate([bias] * s_q, axis=0)

    def attend(s, vt_pages):
        m_prev = m_ref[...]
        m_new = jnp.maximum(m_prev, jnp.max(s, axis=1, keepdims=True))
        p = jnp.exp(s - m_new).astype(BF16)
        alpha = jnp.exp(m_prev - m_new)
        l_ref[...] = alpha * l_ref[...] + jnp.sum(p.astype(F32), axis=1, keepdims=True)
        pv, k0 = 0.0, 0
        for vt in vt_pages:
            pv = pv + _dot_nt(p[:, k0:k0 + vt.shape[1]], vt.astype(BF16))
            k0 += vt.shape[1]
        acc_ref[...] = alpha * acc_ref[...] + pv
        m_ref[...] = m_new

    jn = lax.broadcasted_iota(jnp.int32, (n_new, 2 * n_new), 0)
    sn = lax.broadcasted_iota(jnp.int32, (n_new, 2 * n_new), 1)
    same_row_later = jnp.logical_and(jn > sn, jn // s_q == sn // s_q)
    mine = jnp.logical_and(sn >= n_new, jn // s_q == bi)
    suffix, total = forget_sums(lfn_ref[0], jnp.where(jnp.logical_or(same_row_later, mine), 1.0, 0.0).astype(BF16))
    s_new = logits(kn_ref[0], suffix)
    key_i = lax.broadcasted_iota(jnp.int32, s_new.shape, 1)
    q_i = lax.broadcasted_iota(jnp.int32, s_new.shape, 0) // N_HEADS
    admitted = jnp.logical_and(key_i // s_q == bi, key_i % s_q <= q_i)
    attend(jnp.where(admitted, s_new, NEG), [vn_ref[0]])
    r_ref[...] = total

    def body(state):
        g, _ = state
        slot = (first_slot + g) % 2
        for copy in group_copies(bi, g, slot):
            copy.wait()
        run = r_ref[...]
        biases = []
        for pi in range(npg):
            suffix, total = forget_sums(lfbuf[slot, pi], later_and_all)
            biases.append(suffix + run)
            run = run + total
        r_ref[...] = run
        reach = sb_ref[0] + jnp.concatenate([run + 1e-5 * jnp.abs(run)] * s_q, axis=0) - m_ref[...]
        more = jnp.logical_and(g + 1 < n_groups, jnp.max(reach) >= -SKIP_THRESH_BF16)

        @pl.when(more)
        def _():
            for copy in group_copies(bi, g + 1, 1 - slot):
                copy.start()

        @pl.when(jnp.logical_and(jnp.logical_not(more), bi + 1 < n_batch))
        def _():
            for copy in group_copies(bi + 1, 0, 1 - slot):
                copy.start()

        s = jnp.concatenate([logits(kbuf[slot, pi], biases[pi]) for pi in range(npg)], axis=1)
        attend(s, [vbuf[slot, pi] for pi in range(npg)])
        return g + 1, more.astype(jnp.int32)

    done, _ = lax.while_loop(lambda state: state[1] > 0, body, (jnp.int32(0), jnp.int32(1)))
    first_slot_ref[0] = (first_slot + done) % 2

    out = jnp.where(own_head, acc_ref[...] / l_ref[...], 0.0)
    o_ref[0] = jnp.sum(out.reshape(s_q, N_HEADS, ATT_W), axis=1)


def _sample_attn(pt, s_bound, q, kt_new, vt_new, lft_new, cache_kt, cache_vt, cache_lft, n_pages):
    b, s_q, _ = q.shape
    page = cache_kt.shape[2]
    n_new = kt_new.shape[2]
    npg = PAGES_PER_GROUP
    assert n_pages % npg == 0 and n_new == page

    def per_batch(shape):
        return pl.BlockSpec((1,) + shape, lambda bi, pt_ref, sb_ref: (bi, 0, 0))

    def shared(shape):
        return pl.BlockSpec((1,) + shape, lambda bi, pt_ref, sb_ref: (0, 0, 0))

    in_hbm = pl.BlockSpec(memory_space=pl.ANY)
    rows = s_q * N_HEADS
    grid_spec = pltpu.PrefetchScalarGridSpec(
        num_scalar_prefetch=2,
        grid=(b,),
        in_specs=[per_batch((s_q, ATT_W)), shared((ATT_W, n_new)), shared((ATT_W, n_new)),
                  shared((N_HEADS, n_new)), in_hbm, in_hbm, in_hbm],
        out_specs=per_batch((s_q, ATT_W)),
        scratch_shapes=[pltpu.VMEM((2, npg, ATT_W, page), F32), pltpu.VMEM((2, npg, ATT_W, page), F32),
                        pltpu.VMEM((2, npg, N_HEADS, page), F32), pltpu.SemaphoreType.DMA((2, 3)),
                        pltpu.SMEM((1,), jnp.int32),
                        pltpu.VMEM((rows, ATT_W), BF16), pltpu.VMEM((rows, 1), F32),
                        pltpu.VMEM((rows, 1), F32), pltpu.VMEM((rows, ATT_W), F32),
                        pltpu.VMEM((N_HEADS, page), F32)],
    )
    return pl.pallas_call(
        functools.partial(_sample_attn_kernel, n_pages),
        name="attn_sample",
        grid_spec=grid_spec,
        out_shape=jax.ShapeDtypeStruct((b, s_q, ATT_W), F32),
        compiler_params=_params(("arbitrary",)),
    )(pt, s_bound, q, kt_new, vt_new, lft_new, cache_kt, cache_vt, cache_lft)


def _layer_weights(l, w):
    wmi = w["w_mix_in"][l]
    d = wmi.shape[0]
    n_qkv = 3 * ATT_W
    pad_rows = BF16_ROWS - N_HEADS
    wt = jnp.concatenate([wmi[:, ATT_W:n_qkv + N_HEADS].T, jnp.zeros((pad_rows, d), F32)], axis=0)
    wn = jnp.concatenate([wmi[:, :ATT_W], wmi[:, n_qkv + N_HEADS:]], axis=1)
    bfp = jnp.broadcast_to(w["b_forget"][l][:, None], (N_HEADS, LANES))
    kgt = jnp.broadcast_to(jnp.tile(w["k_norm"][l], N_HEADS)[:, None], (ATT_W, LANES))
    head_of = jnp.arange(ATT_W) // HEAD_DIM
    gmat = jnp.where(head_of[:, None] == head_of[None, :], 1.0 / HEAD_DIM, 0.0).astype(BF16)
    s_bound = (HEAD_DIM ** 0.5) * jnp.max(jnp.abs(w["q_norm"][l])) * jnp.max(jnp.abs(w["k_norm"][l])) * 1.02
    wmo = w["w_mix_out"][l].astype(BF16)
    return dict(
        layer=l, ffn1_norm=w["ffn1_norm"][l], ffn1_w_in=w["ffn1_w_in_bf"], ffn1_w_out=w["ffn1_w_out_bf"],
        mix_norm=w["mix_norm"][l], wt=wt.astype(BF16), wn=wn.astype(BF16), bfp=bfp,
        qg=jnp.tile(w["q_norm"][l], N_HEADS).reshape(1, ATT_W), kgt=kgt, gmat=gmat,
        s_bound=s_bound.reshape(1).astype(F32),
        conv_w=w["conv_w"][l], conv_b=w["conv_b"][l], conv_ln_g=w["conv_ln_g"][l],
        conv_ln_b=w["conv_ln_b"][l], wmo_a=wmo[:ATT_W], wmo_c=wmo[ATT_W:],
        ffn2_norm=w["ffn2_norm"][l], ffn2_w_in=w["ffn2_w_in_bf"], ffn2_w_out=w["ffn2_w_out_bf"],
        ple_norm=w["ple_norm"][l], wg=w["w_ple_gate"][l].astype(BF16), wp=w["w_ple_proj"][l].astype(BF16),
    )


def _mixin_call(x3, lw, tm, prev=()):
    return _mixin(x3, lw["mix_norm"], lw["wt"], lw["wn"], lw["bfp"], lw["qg"], lw["kgt"], lw["gmat"], tm, prev)


def kernel(x_prompt, x_sample, p_prompt, p_sample, cache_k, cache_v, cache_logf, state_conv, page_table, ffn1_norm, ffn1_w_in, ffn1_w_out, mix_norm, w_mix_in, b_forget, q_norm, k_norm, conv_w, conv_b, conv_ln_g, conv_ln_b, w_mix_out, ffn2_norm, ffn2_w_in, ffn2_w_out, ple_norm, w_ple_gate, w_ple_proj):
    depth = ffn1_norm.shape[0]
    b, t, d = x_prompt.shape
    sb, ss, _ = x_sample.shape
    n_pool, page = cache_k.shape[1], cache_k.shape[2]
    n_pages = page_table.shape[1]
    hist = state_conv.shape[2]
    ms = sb * ss

    w = dict(ffn1_norm=ffn1_norm, mix_norm=mix_norm, w_mix_in=w_mix_in, b_forget=b_forget, q_norm=q_norm,
             k_norm=k_norm, conv_w=conv_w, conv_b=conv_b, conv_ln_g=conv_ln_g, conv_ln_b=conv_ln_b,
             w_mix_out=w_mix_out, ffn2_norm=ffn2_norm, ple_norm=ple_norm, w_ple_gate=w_ple_gate,
             w_ple_proj=w_ple_proj,
             ffn1_w_in_bf=ffn1_w_in.astype(BF16), ffn1_w_out_bf=ffn1_w_out.astype(BF16),
             ffn2_w_in_bf=ffn2_w_in.astype(BF16), ffn2_w_out_bf=ffn2_w_out.astype(BF16))

    ckt = jnp.transpose(cache_k, (0, 1, 3, 4, 2)).reshape(depth * n_pool, ATT_W, page)
    cvt = jnp.transpose(cache_v, (0, 1, 3, 4, 2)).reshape(depth * n_pool, ATT_W, page)
    clft = jnp.transpose(cache_logf, (0, 1, 3, 2)).reshape(depth * n_pool, N_HEADS, page)
    state_t = jnp.transpose(state_conv, (0, 2, 1, 3))

    xp = x_prompt.reshape(b * t, d)
    xs = x_sample.reshape(ms, d)
    outs = {name: [] for name in ("fp", "cp", "ks", "vs", "fs", "cs")}
    kv_prompt = ()

    for l in range(depth):
        lw = _layer_weights(l, w)

        xp = _ffn(xp, lw["ffn1_norm"], lw["ffn1_w_in"], lw["ffn1_w_out"], l, FFN_ROW_TILE)
        qb, ktb, vb, kt, vt, lft, ct, cs, u = _mixin_call(xp.reshape(b, t, d), lw, ROW_TILE, kv_prompt)
        kv_prompt = (kt, vt)
        attn = _attn_prompt(qb, ktb, vb, ct, cs, lw["s_bound"], ATT_TILE)
        conv = _conv_prompt(u, lw["conv_w"], lw["conv_b"], ROW_TILE)
        xp = _post(xp, attn.reshape(b * t, ATT_W), conv.reshape(b * t, -1),
                   p_prompt[l].reshape(b * t, -1), lw, ROW_TILE)
        outs["fp"].append(jnp.transpose(lft, (0, 2, 1)))
        outs["cp"].append(u[:, t - hist:, :])

        xs = _ffn(xs, lw["ffn1_norm"], lw["ffn1_w_in"], lw["ffn1_w_out"], l, ms)
        qb, _, _, kt, vt, lft, _, _, u = _mixin_call(xs.reshape(1, ms, d), lw, ms)
        kt, vt = kt[0], vt[0]
        pt = (page_table + l * n_pool).reshape(-1)
        attn = _sample_attn(pt, lw["s_bound"], qb.reshape(sb, ss, ATT_W).astype(F32), kt, vt, lft,
                            ckt, cvt, clft, n_pages)
        u_t = jnp.transpose(u.reshape(sb, ss, -1), (1, 0, 2))
        conv = _conv_sample(state_t[l], u_t, lw["conv_w"], lw["conv_b"])
        conv = jnp.transpose(conv, (1, 0, 2)).reshape(ms, -1)
        xs = _post(xs, attn.reshape(ms, ATT_W).astype(BF16), conv, p_sample[l].reshape(ms, -1), lw, ms)
        outs["ks"].append(kt[0].T.reshape(sb, ss, N_HEADS, HEAD_DIM))
        outs["vs"].append(vt[0].T.reshape(sb, ss, N_HEADS, HEAD_DIM))
        outs["fs"].append(lft[0].T.reshape(sb, ss, N_HEADS))
        outs["cs"].append(jnp.transpose(jnp.concatenate([state_t[l], u_t], axis=0)[ss:], (1, 0, 2)))

    stack = lambda name: jnp.stack(outs[name])
    heads_last = lambda a: jnp.transpose(a.reshape(depth, b, N_HEADS, HEAD_DIM, t), (0, 1, 4, 2, 3))
    return (xp.reshape(b, t, d), xs.reshape(sb, ss, d), heads_last(kv_prompt[0]), heads_last(kv_prompt[1]),
            stack("fp"), stack("cp"), stack("ks"), stack("vs"), stack("fs"), stack("cs"))
```
